```python
import math, functools
import jax, jax.numpy as jnp
from jax import lax
import numpy as np

D_MODEL = 1024
BATCH = 2
SEQ = 8192
DEPTH = 2
DEC_BATCH = 128
DEC_SEQ = 1
PAST_LEN = 2048
PAGE_SIZE = 128

HEAD_DIM = 64
H_SB = 6
H_MB = 6
H_MEM = 4
D_SB = H_SB * HEAD_DIM
D_MB = H_MB * HEAD_DIM
D_MEM = H_MEM * HEAD_DIM
D_MIX = D_SB + D_MB + D_MEM
D_PROJ = 4 * D_SB + 4 * D_MB + 2 * D_MEM
N_MEM = 256
MOBA_BLOCK = 256
MOBA_TOPK = 3
Q_BLOCK = 128
NUM_BUCKETS = 32
MAX_EXACT = 16
MAX_DISTANCE = 128
RMS_EPS = 1e-6
NEG_INF = -1e30

kernel_name = "hybrid_stickbreak_moba_memxattn_step"


def rms_norm(x, g):
    xf = x.astype(jnp.float32)
    y = xf * lax.rsqrt(jnp.mean(xf * xf, axis=-1, keepdims=True) + RMS_EPS)
    return (y * g.astype(jnp.float32)).astype(x.dtype)


def t5_bucket(dist):
    n = jnp.maximum(dist, 0)
    nf = jnp.maximum(n, 1).astype(jnp.float32)
    large = MAX_EXACT + (jnp.log(nf / MAX_EXACT) / math.log(MAX_DISTANCE / MAX_EXACT)
                         * (NUM_BUCKETS - MAX_EXACT)).astype(jnp.int32)
    large = jnp.minimum(large, NUM_BUCKETS - 1)
    return jnp.where(n < MAX_EXACT, n, large)


def project(h, w_in):
    p = jnp.einsum('btd,de->bte', h, w_in)
    B, T, _ = p.shape
    sizes = (D_SB,) * 4 + (D_MB,) * 4 + (D_MEM,) * 2
    heads = (H_SB,) * 4 + (H_MB,) * 4 + (H_MEM,) * 2
    parts = []
    off = 0
    for s, nh in zip(sizes, heads):
        parts.append(p[..., off:off + s].reshape(B, T, nh, HEAD_DIM))
        off += s
    return parts


def mem_kv(mem, g_mem, w_mem_kv):
    hm = rms_norm(mem, g_mem)
    kv = jnp.einsum('bmd,de->bme', hm, w_mem_kv)
    B, M, _ = kv.shape
    return (kv[..., :D_MEM].reshape(B, M, H_MEM, HEAD_DIM),
            kv[..., D_MEM:].reshape(B, M, H_MEM, HEAD_DIM))


def sweep(fn, q, q_pos):
    B, T, H, D = q.shape
    blk = Q_BLOCK if T % Q_BLOCK == 0 else T
    n = T // blk
    qb = q.reshape(B, n, blk, H, D).swapaxes(0, 1)
    pb = q_pos.reshape(n, blk)
    o = lax.map(lambda a: fn(a[0], a[1]), (qb, pb))
    return o.swapaxes(0, 1).reshape(B, T, H, D)


def sb_block(q, qp, k, v):
    L = k.shape[1]
    z = jnp.einsum('bqhd,bkhd->bhqk', q.astype(jnp.float32), k.astype(jnp.float32)) / math.sqrt(HEAD_DIM)
    kpos = jnp.arange(L, dtype=jnp.int32)
    mask = kpos[None, :] < qp[:, None]
    log_keep = jnp.where(mask, jax.nn.log_sigmoid(-z), 0.0)
    suffix = lax.cumsum(log_keep, axis=3, reverse=True) - log_keep
    a = jnp.where(mask, jnp.exp(jax.nn.log_sigmoid(z) + suffix), 0.0)
    return jnp.einsum('bhqk,bkhd->bqhd', a, v.astype(jnp.float32)).astype(q.dtype)


def stick_breaking(q, q_pos, k, v):
    return sweep(lambda qb, pb: sb_block(qb, pb, k, v), q, q_pos)


def moba_block(q, qp, kb, vb, kmean, rel_bias):
    B, Q, H, D = q.shape
    NB = kb.shape[2]
    scale = 1.0 / math.sqrt(HEAD_DIM)
    qf = q.astype(jnp.float32).transpose(0, 2, 1, 3)
    c = qp // MOBA_BLOCK
    gate = jnp.einsum('bhqd,bhnd->bhqn', qf, kmean)
    past = jnp.arange(NB)[None, :] < c[:, None]
    gate = jnp.where(past, gate, NEG_INF)
    ksel = min(MOBA_TOPK, NB)
    _, idx = lax.top_k(gate, ksel)
    valid = jnp.arange(ksel)[None, :] < c[:, None]
    bi = jnp.arange(B)[:, None, None, None]
    hi = jnp.arange(H)[None, :, None, None]
    kg = kb[bi, hi, idx].astype(jnp.float32)
    vg = vb[bi, hi, idx].astype(jnp.float32)
    tab = rel_bias.astype(jnp.float32).T
    kpos_g = idx[..., None] * MOBA_BLOCK + jnp.arange(MOBA_BLOCK, dtype=jnp.int32)
    bias_p = tab[hi[..., None], t5_bucket(qp[None, None, :, None, None] - kpos_g)]
    lp = jnp.einsum('bhqd,bhqjkd->bhqjk', qf, kg) * scale + bias_p
    lp = jnp.where(valid[None, None, :, :, None], lp, NEG_INF).reshape(B, H, Q, ksel * MOBA_BLOCK)
    c0 = qp[0] // MOBA_BLOCK
    ko = lax.dynamic_index_in_dim(kb, c0, axis=2, keepdims=False).astype(jnp.float32)
    vo = lax.dynamic_index_in_dim(vb, c0, axis=2, keepdims=False).astype(jnp.float32)
    kpos_o = c0 * MOBA_BLOCK + jnp.arange(MOBA_BLOCK, dtype=jnp.int32)
    bias_o = tab[:, t5_bucket(qp[:, None] - kpos_o[None, :])]
    lo = jnp.einsum('bhqd,bhkd->bhqk', qf, ko) * scale + bias_o
    lo = jnp.where(kpos_o[None, :] <= qp[:, None], lo, NEG_INF)
    p = jax.nn.softmax(jnp.concatenate([lp, lo], axis=-1), axis=-1)
    pp = p[..., :ksel * MOBA_BLOCK].reshape(B, H, Q, ksel, MOBA_BLOCK)
    po = p[..., ksel * MOBA_BLOCK:]
    o = jnp.einsum('bhqjk,bhqjkd->bhqd', pp, vg) + jnp.einsum('bhqk,bhkd->bhqd', po, vo)
    return o.transpose(0, 2, 1, 3).astype(q.dtype)


def moba(q, q_pos, k, v, rel_bias):
    B, L, H, D = k.shape
    pad = (-L) % MOBA_BLOCK
    nb = (L + pad) // MOBA_BLOCK
    kb = jnp.pad(k, ((0, 0), (0, pad), (0, 0), (0, 0))).reshape(B, nb, MOBA_BLOCK, H, D).transpose(0, 3, 1, 2, 4)
    vb = jnp.pad(v, ((0, 0), (0, pad), (0, 0), (0, 0))).reshape(B, nb, MOBA_BLOCK, H, D).transpose(0, 3, 1, 2, 4)
    kmean = jnp.mean(kb.astype(jnp.float32), axis=3)
    return sweep(lambda qb, pb: moba_block(qb, pb, kb, vb, kmean, rel_bias), q, q_pos)


def mem_attend(q, mk, mv):
    s = jnp.einsum('bthd,bmhd->bhtm', q.astype(jnp.float32), mk.astype(jnp.float32)) / math.sqrt(HEAD_DIM)
    p = jax.nn.softmax(s, axis=-1)
    return jnp.einsum('bhtm,bmhd->bthd', p, mv.astype(jnp.float32)).astype(q.dtype)


def merge(x, outs, gates, w_out, g_post):
    B, T, _ = x.shape
    mixed = jnp.concatenate([(o * jax.nn.silu(z)).reshape(B, T, -1) for o, z in zip(outs, gates)], axis=-1)
    y = jnp.einsum('bte,ed->btd', mixed, w_out)
    return x + rms_norm(y, g_post)


def gather_pages(cache_l, page_table):
    pages = cache_l[page_table]
    db, n_p, ps, h, d = pages.shape
    return pages.reshape(db, n_p * ps, h, d)


def prompt_layer(x, mem, rel_bias, g_pre, g_post, g_mem, w_in, w_out, w_mem_kv):
    T = x.shape[1]
    h = rms_norm(x, g_pre)
    q_sb, k_sb, v_sb, z_sb, q_mb, k_mb, v_mb, z_mb, q_mem, z_mem = project(h, w_in)
    mk, mv = mem_kv(mem, g_mem, w_mem_kv)
    pos = jnp.arange(T, dtype=jnp.int32)
    o_sb = stick_breaking(q_sb, pos, k_sb, v_sb)
    o_mb = moba(q_mb, pos, k_mb, v_mb, rel_bias)
    o_mem = mem_attend(q_mem, mk, mv)
    x = merge(x, (o_sb, o_mb, o_mem), (z_sb, z_mb, z_mem), w_out, g_post)
    return x, k_sb, v_sb, k_mb, v_mb, mk, mv


def sample_layer(x, c_sb_k, c_sb_v, c_mb_k, c_mb_v, c_mem_k, c_mem_v, page_table,
                 rel_bias, g_pre, g_post, w_in, w_out):
    T = x.shape[1]
    h = rms_norm(x, g_pre)
    q_sb, k_sb, v_sb, z_sb, q_mb, k_mb, v_mb, z_mb, q_mem, z_mem = project(h, w_in)
    pos = PAST_LEN + jnp.arange(T, dtype=jnp.int32)
    k_sb_all = jnp.concatenate([gather_pages(c_sb_k, page_table).astype(k_sb.dtype), k_sb], axis=1)
    v_sb_all = jnp.concatenate([gather_pages(c_sb_v, page_table).astype(v_sb.dtype), v_sb], axis=1)
    k_mb_all = jnp.concatenate([gather_pages(c_mb_k, page_table).astype(k_mb.dtype), k_mb], axis=1)
    v_mb_all = jnp.concatenate([gather_pages(c_mb_v, page_table).astype(v_mb.dtype), v_mb], axis=1)
    o_sb = stick_breaking(q_sb, pos, k_sb_all, v_sb_all)
    o_mb = moba(q_mb, pos, k_mb_all, v_mb_all, rel_bias)
    o_mem = mem_attend(q_mem, c_mem_k, c_mem_v)
    x = merge(x, (o_sb, o_mb, o_mem), (z_sb, z_mb, z_mem), w_out, g_post)
    return x, k_sb, v_sb, k_mb, v_mb


def setup_inputs(seed: int = 0) -> dict:
    key = jax.random.key(seed)
    ks = jax.random.split(key, 20)
    n_pages = PAST_LEN // PAGE_SIZE
    n_used = DEC_BATCH * n_pages
    n_phys = n_used + (n_used + 3) // 4
    nrm = lambda k, s: jax.random.normal(k, s, dtype=jnp.float32)
    x_prompt = nrm(ks[0], (BATCH, SEQ, D_MODEL))
    x_sample = nrm(ks[1], (DEC_BATCH, DEC_SEQ, D_MODEL))
    cache_sb_k = nrm(ks[2], (DEPTH, n_phys, PAGE_SIZE, H_SB, HEAD_DIM))
    cache_sb_v = nrm(ks[3], (DEPTH, n_phys, PAGE_SIZE, H_SB, HEAD_DIM))
    cache_moba_k = nrm(ks[4], (DEPTH, n_phys, PAGE_SIZE, H_MB, HEAD_DIM))
    cache_moba_v = nrm(ks[5], (DEPTH, n_phys, PAGE_SIZE, H_MB, HEAD_DIM))
    cache_mem_k = nrm(ks[6], (DEPTH, DEC_BATCH, N_MEM, H_MEM, HEAD_DIM))
    cache_mem_v = nrm(ks[7], (DEPTH, DEC_BATCH, N_MEM, H_MEM, HEAD_DIM))
    page_table = jax.random.permutation(ks[8], n_phys)[:n_used].reshape(DEC_BATCH, n_pages).astype(jnp.int32)
    mem_prompt = nrm(ks[9], (BATCH, N_MEM, D_MODEL))
    rel_bias = 0.5 * nrm(ks[10], (NUM_BUCKETS, H_MB))
    g_pre = 1.0 + 0.05 * nrm(ks[11], (DEPTH, D_MODEL))
    g_post = 1.0 + 0.05 * nrm(ks[12], (DEPTH, D_MODEL))
    g_mem = 1.0 + 0.05 * nrm(ks[13], (DEPTH, D_MODEL))
    w_in = nrm(ks[14], (DEPTH, D_MODEL, D_PROJ)) * D_MODEL ** -0.5
    w_out = nrm(ks[15], (DEPTH, D_MIX, D_MODEL)) * D_MIX ** -0.5
    w_mem_kv = nrm(ks[16], (DEPTH, D_MODEL, 2 * D_MEM)) * D_MODEL ** -0.5
    return {"x_prompt": x_prompt, "x_sample": x_sample,
            "cache_sb_k": cache_sb_k, "cache_sb_v": cache_sb_v,
            "cache_moba_k": cache_moba_k, "cache_moba_v": cache_moba_v,
            "cache_mem_k": cache_mem_k, "cache_mem_v": cache_mem_v,
            "page_table": page_table, "mem_prompt": mem_prompt,
            "rel_bias": rel_bias, "g_pre": g_pre, "g_post": g_post, "g_mem": g_mem,
            "w_in": w_in, "w_out": w_out, "w_mem_kv": w_mem_kv}


def reference(x_prompt, x_sample, cache_sb_k, cache_sb_v, cache_moba_k, cache_moba_v,
              cache_mem_k, cache_mem_v, page_table, mem_prompt, rel_bias, g_pre, g_post,
              g_mem, w_in, w_out, w_mem_kv):
    xp = x_prompt
    xs = x_sample
    p_sbk, p_sbv, p_mbk, p_mbv, p_mk, p_mv = [], [], [], [], [], []
    s_sbk, s_sbv, s_mbk, s_mbv = [], [], [], []
    for l in range(DEPTH):
        xp, ksb, vsb, kmb, vmb, mk, mv = prompt_layer(
            xp, mem_prompt, rel_bias, g_pre[l], g_post[l], g_mem[l], w_in[l], w_out[l], w_mem_kv[l])
        p_sbk.append(ksb); p_sbv.append(vsb); p_mbk.append(kmb); p_mbv.append(vmb)
        p_mk.append(mk); p_mv.append(mv)
        xs, ksb_s, vsb_s, kmb_s, vmb_s = sample_layer(
            xs, cache_sb_k[l], cache_sb_v[l], cache_moba_k[l], cache_moba_v[l],
            cache_mem_k[l], cache_mem_v[l], page_table, rel_bias, g_pre[l], g_post[l], w_in[l], w_out[l])
        s_sbk.append(ksb_s); s_sbv.append(vsb_s); s_mbk.append(kmb_s); s_mbv.append(vmb_s)
    return (xp, xs,
            jnp.stack(p_sbk), jnp.stack(p_sbv), jnp.stack(p_mbk), jnp.stack(p_mbv),
            jnp.stack(p_mk), jnp.stack(p_mv),
            jnp.stack(s_sbk), jnp.stack(s_sbv), jnp.stack(s_mbk), jnp.stack(s_mbv))
```

```python
import functools
import math

import numpy as np
import jax
import jax.numpy as jnp
from jax import lax
from jax.experimental import pallas as pl
from jax.experimental.pallas import tpu as pltpu

HEAD_DIM = 64
H_SB, H_MB, H_MEM = 6, 6, 4
D_SB, D_MB, D_MEM = H_SB * HEAD_DIM, H_MB * HEAD_DIM, H_MEM * HEAD_DIM
D_MIX = D_SB + D_MB + D_MEM
MOBA_BLOCK = 256
MOBA_TOPK = 3
PAGE_SIZE = 128
NUM_BUCKETS = 32
MAX_EXACT = 16
MAX_DISTANCE = 128
RMS_EPS = 1e-6
NEG_INF = -1e30
Q_SCALE = 1.0 / math.sqrt(HEAD_DIM)

LANES = 128
SUBLANES = 8
PAIR = LANES // HEAD_DIM
VMEM_LIMIT = 48 * 1024 * 1024
TOKEN_TILE = MOBA_BLOCK

SB_EXIT = 110.0

BF16 = jnp.bfloat16
F32 = jnp.float32
_NT = (((1,), (1,)), ((), ()))


def _cparams(n_axes):
    return pltpu.CompilerParams(dimension_semantics=("arbitrary",) * n_axes,
                                vmem_limit_bytes=VMEM_LIMIT)


def _dot(a, b):
    return jnp.dot(a, b, preferred_element_type=F32)


def _dot_nt(a, b):
    return lax.dot_general(a, b, _NT, preferred_element_type=F32)


def _split_bf16(x):
    hi = x.astype(BF16)
    return hi, (x - hi.astype(F32)).astype(BF16)


def _split_dot(x, u):
    hi, lo = _split_bf16(x)
    return _dot(hi, u) + _dot(lo, u)


def _head_lane_mask(shape, hh):
    lane = lax.broadcasted_iota(jnp.int32, shape, len(shape) - 1)
    return (lane >= HEAD_DIM) if hh else (lane < HEAD_DIM)


def _softplus_parts(z):
    t = jnp.log1p(jnp.exp(-jnp.abs(z)))
    return jnp.minimum(-z, 0.0) - t, jnp.minimum(z, 0.0) - t


def _norm_proj_body(*refs, plan, has_w, has_wt):
    x_ref, g_ref = refs[:2]
    rest = list(refs[2:])
    w_ref = rest.pop(0) if has_w else None
    wt_ref = rest.pop(0) if has_wt else None
    out_refs = rest
    x = x_ref[...]
    ms = jnp.mean(x * x, axis=-1, keepdims=True)
    h = (x * lax.rsqrt(ms + RMS_EPS) * g_ref[...]).astype(BF16)
    for (transposed, lo, hi), writes in plan:
        p = _dot_nt(wt_ref[lo:hi, :], h) if transposed else _dot(h, w_ref[:, lo:hi])
        for oi, dlo, kind in writes:
            if kind == "f32":
                val = p
            elif kind == "bf16":
                val = p.astype(BF16)
            elif kind == "bf16_scaled":
                val = (p * Q_SCALE).astype(BF16)
            else:
                val = p * (1.0 / (1.0 + jnp.exp(-p)))
            if transposed:
                out_refs[oi][0] = val
            else:
                out_refs[oi][:, dlo:dlo + (hi - lo)] = val


def _norm_proj(x2, gain, w, wt, outs, plan, tm, tiles_per_batch):
    n, d = x2.shape
    nt = n // tm
    in_specs = [pl.BlockSpec((tm, d), lambda i: (i, 0)), pl.BlockSpec((1, d), lambda i: (0, 0))]
    args = [x2, gain.reshape(1, d)]
    for m in (w, wt):
        if m is not None:
            in_specs.append(pl.BlockSpec(m.shape, lambda i: (0, 0)))
            args.append(m)
    out_specs, out_shape = [], []
    for kind, width, dt in outs:
        if kind == "rows":
            out_specs.append(pl.BlockSpec((tm, width), lambda i: (i, 0)))
            out_shape.append(jax.ShapeDtypeStruct((n, width), dt))
        elif kind == "t_batch":
            out_specs.append(pl.BlockSpec((1, width, tm),
                                          lambda i: (i // tiles_per_batch, 0, i % tiles_per_batch)))
            out_shape.append(jax.ShapeDtypeStruct((nt // tiles_per_batch, width, tiles_per_batch * tm), dt))
        else:
            out_specs.append(pl.BlockSpec((1, width, tm), lambda i: (i, 0, 0)))
            out_shape.append(jax.ShapeDtypeStruct((nt, width, tm), dt))
    return pl.pallas_call(
        functools.partial(_norm_proj_body, plan=plan, has_w=w is not None, has_wt=wt is not None),
        grid=(nt,),
        in_specs=in_specs,
        out_specs=out_specs,
        out_shape=out_shape,
        compiler_params=_cparams(1),
        name="norm_proj",
    )(*args)


_SEG = {}
_c = 0
for _name, _w in (("q_sb", D_SB), ("k_sb", D_SB), ("v_sb", D_SB), ("z_sb", D_SB),
                  ("q_mb", D_MB), ("k_mb", D_MB), ("v_mb", D_MB), ("z_mb", D_MB),
                  ("q_mem", D_MEM), ("z_mem", D_MEM)):
    _SEG[_name] = (_c, _c + _w)
    _c += _w
_KV_NAMES = ("k_sb", "v_sb", "k_mb", "v_mb")


def _kv_weight_t(w_in_l):
    return jnp.concatenate([w_in_l[:, _SEG[s][0]:_SEG[s][1]] for s in _KV_NAMES], axis=1).T.astype(BF16)


def _prompt_proj_plan():
    outs = [("t_batch", D_SB, F32)] * 4 + [("t_tile", D_SB, BF16)] * 4
    outs += [("rows", D_SB, BF16), ("rows", D_MB, F32), ("rows", D_MEM, BF16),
             ("rows", D_MIX, F32)]
    plan = [((True, i * D_SB, (i + 1) * D_SB), ((i, 0, "f32"), (4 + i, 0, "bf16"))) for i in range(4)]
    plan += [((False,) + _SEG["q_sb"], ((8, 0, "bf16_scaled"),)),
             ((False,) + _SEG["q_mb"], ((9, 0, "f32"),)),
             ((False,) + _SEG["q_mem"], ((10, 0, "bf16_scaled"),)),
             ((False,) + _SEG["z_sb"], ((11, 0, "silu"),)),
             ((False,) + _SEG["z_mb"], ((11, D_SB, "silu"),)),
             ((False,) + _SEG["z_mem"], ((11, D_SB + D_MB, "silu"),))]
    return outs, tuple(plan)


def _sample_proj_plan():
    outs = [("t_batch", D_SB, F32)] * 4
    outs += [("rows", D_MB, F32), ("rows", D_MB, F32)]
    outs += [("rows", D_SB, BF16), ("rows", D_MB, F32), ("rows", D_MEM, BF16),
             ("rows", D_MIX, F32)]
    plan = [((True, i * D_SB, (i + 1) * D_SB), ((i, 0, "f32"),)) for i in range(4)]
    plan += [((False,) + _SEG["k_mb"], ((4, 0, "f32"),)),
             ((False,) + _SEG["v_mb"], ((5, 0, "f32"),)),
             ((False,) + _SEG["q_sb"], ((6, 0, "bf16_scaled"),)),
             ((False,) + _SEG["q_mb"], ((7, 0, "f32"),)),
             ((False,) + _SEG["q_mem"], ((8, 0, "bf16_scaled"),)),
             ((False,) + _SEG["z_sb"], ((9, 0, "silu"),)),
             ((False,) + _SEG["z_mb"], ((9, D_SB, "silu"),)),
             ((False,) + _SEG["z_mem"], ((9, D_SB + D_MB, "silu"),))]
    return outs, tuple(plan)


def _mem_kv_plan():
    outs = [("t_tile", D_MEM, F32), ("t_tile", D_MEM, F32), ("t_tile", D_MEM, BF16), ("t_tile", D_MEM, BF16)]
    plan = (((True, 0, D_MEM), ((0, 0, "f32"), (2, 0, "bf16"))),
            ((True, D_MEM, 2 * D_MEM), ((1, 0, "f32"), (3, 0, "bf16"))))
    return outs, plan


def _merge_body(msb_ref, mmb_ref, mmem_ref, x_ref, w_ref, g_ref, o_ref):
    y = (_dot(msb_ref[...], w_ref[0:D_SB, :])
         + _dot(mmb_ref[...], w_ref[D_SB:D_SB + D_MB, :])
         + _dot(mmem_ref[...], w_ref[D_SB + D_MB:D_MIX, :]))
    ms = jnp.mean(y * y, axis=-1, keepdims=True)
    o_ref[...] = x_ref[...] + y * lax.rsqrt(ms + RMS_EPS) * g_ref[...]


def _merge(msb, mmb, mmem, x2, w_out_bf16, g_post, tm):
    n, d = x2.shape
    row = lambda w: pl.BlockSpec((tm, w), lambda i: (i, 0))
    return pl.pallas_call(
        _merge_body,
        grid=(n // tm,),
        in_specs=[row(D_SB), row(D_MB), row(D_MEM), row(d),
                  pl.BlockSpec((D_MIX, d), lambda i: (0, 0)),
                  pl.BlockSpec((1, d), lambda i: (0, 0))],
        out_specs=row(d),
        out_shape=jax.ShapeDtypeStruct((n, d), F32),
        compiler_params=_cparams(1),
        name="merge",
    )(msb, mmb, mmem, x2, w_out_bf16, g_post.reshape(1, d))


def _sb_prompt_body(q_ref, kt_ref, vt_ref, u_ref, g_ref, o_ref):
    tq = TOKEN_TILE
    qi = pl.program_id(2)
    q = q_ref[...]
    row = lax.broadcasted_iota(jnp.int32, (tq, tq), 0)
    col = lax.broadcasted_iota(jnp.int32, (tq, tq), 1)
    rel = col - row
    out = jnp.zeros((tq, LANES), F32)
    for hh in range(PAIR):
        head = _head_lane_mask((tq, LANES), hh)
        qm = jnp.where(head, q, jnp.zeros_like(q))

        def cond(st):
            j, cmax, _, _ = st
            return jnp.logical_and(j >= 0, cmax > -SB_EXIT)

        def body(st, qm=qm):
            j, _, carry, acc = st
            z = _dot(qm, kt_ref[j])
            log_keep, log_beta = _softplus_parts(z)
            mask = rel < (qi - j) * tq
            log_keep = jnp.where(mask, log_keep, 0.0)
            suffix = _split_dot(log_keep, u_ref[...])
            a = jnp.where(mask, jnp.exp(log_beta + suffix + carry), 0.0)
            acc = acc + _dot_nt(a.astype(BF16), vt_ref[j])
            carry = carry + suffix[:, 0:1] + log_keep[:, 0:1]
            return j - 1, jnp.max(carry), carry, acc

        init = (qi, jnp.float32(0.0), jnp.zeros((tq, 1), F32), jnp.zeros((tq, LANES), F32))
        _, _, _, acc = lax.while_loop(cond, body, init)
        out = jnp.where(head, acc, out)
    o_ref[...] = (out * g_ref[...]).astype(BF16)


def _prompt_specs(nq, g_off):
    tq = TOKEN_TILE
    tile = lambda off: pl.BlockSpec((tq, LANES), lambda b, p, i: (b * nq + i, p + off))
    keys = pl.BlockSpec((nq, LANES, tq), lambda b, p, i: (b, p, 0))
    return tile(0), keys, tile(g_off)


def _sb_prompt(qb, ktb, vtb, gates, u, batch, seq):
    tq = TOKEN_TILE
    nq = seq // tq
    tile, keys, gate_tile = _prompt_specs(nq, 0)
    return pl.pallas_call(
        _sb_prompt_body,
        grid=(batch, D_SB // LANES, nq),
        in_specs=[tile, keys, keys, pl.BlockSpec((tq, tq), lambda b, p, i: (0, 0)), gate_tile],
        out_specs=tile,
        out_shape=jax.ShapeDtypeStruct((batch * seq, D_SB), BF16),
        compiler_params=_cparams(3),
        name="sb_prompt",
    )(qb, ktb, vtb, u, gates)


def _kmean_body(kt_ref, o_ref):
    nb = pl.program_id(1)
    mean = jnp.sum(kt_ref[0], axis=1, keepdims=True) * (1.0 / MOBA_BLOCK)
    lane = lax.broadcasted_iota(jnp.int32, o_ref.shape[1:], 1)

    @pl.when(nb == 0)
    def _():
        o_ref[0] = jnp.zeros(o_ref.shape[1:], F32)

    o_ref[0] = jnp.where(lane == nb, mean, o_ref[0])


def _kmean(kt_f32, batch, seq):
    nb = seq // MOBA_BLOCK
    assert nb <= LANES
    return pl.pallas_call(
        _kmean_body,
        grid=(batch, nb),
        in_specs=[pl.BlockSpec((1, D_MB, MOBA_BLOCK), lambda b, n: (b, 0, n))],
        out_specs=pl.BlockSpec((1, D_MB, LANES), lambda b, n: (b, 0, 0)),
        out_shape=jax.ShapeDtypeStruct((batch, D_MB, LANES), F32),
        compiler_params=_cparams(2),
        name="kmean",
    )(kt_f32)


def _t5_bucket_np(dist):
    n = np.maximum(dist, 0)
    nf = np.maximum(n, 1).astype(np.float32)
    large = MAX_EXACT + (np.log(nf / np.float32(MAX_EXACT)) / np.float32(math.log(MAX_DISTANCE / MAX_EXACT))
                         * np.float32(NUM_BUCKETS - MAX_EXACT)).astype(np.int32)
    large = np.minimum(large, NUM_BUCKETS - 1)
    return np.where(n < MAX_EXACT, n, large).astype(np.int32)


def _bias_tiles_body(tab_ref, bucket_ref, o_ref):
    h = pl.program_id(0)
    for t in range(2):
        bucket = bucket_ref[t]
        acc = jnp.zeros(bucket.shape, F32)
        for b in range(NUM_BUCKETS):
            acc = jnp.where(bucket == b, tab_ref[b, h], acc)
        o_ref[0, t] = acc


def _bias_tiles(rel_bias):
    i = np.arange(MOBA_BLOCK)[:, None]
    j = np.arange(MOBA_BLOCK)[None, :]
    buckets = np.stack([_t5_bucket_np(i - j), _t5_bucket_np(MOBA_BLOCK + i - j)])
    return pl.pallas_call(
        _bias_tiles_body,
        grid=(H_MB,),
        in_specs=[pl.BlockSpec(memory_space=pltpu.SMEM),
                  pl.BlockSpec((2, MOBA_BLOCK, MOBA_BLOCK), lambda h: (0, 0, 0))],
        out_specs=pl.BlockSpec((1, 2, MOBA_BLOCK, MOBA_BLOCK), lambda h: (h, 0, 0, 0)),
        out_shape=jax.ShapeDtypeStruct((H_MB, 2, MOBA_BLOCK, MOBA_BLOCK), F32),
        compiler_params=_cparams(1),
        name="bias_tiles",
    )(rel_bias, jnp.asarray(buckets))


def _select_blocks(gate, n_past):
    lane = lax.broadcasted_iota(jnp.int32, gate.shape, 1)
    lane_f = lane.astype(F32)
    g = jnp.where(lane < n_past, gate, NEG_INF)
    sel = jnp.zeros(gate.shape, F32)
    for _ in range(MOBA_TOPK):
        m = jnp.max(g, axis=1, keepdims=True)
        first = jnp.min(jnp.where(g == m, lane_f, float(LANES)), axis=1, keepdims=True)
        pick = lane_f == first
        sel = jnp.where(pick, 1.0, sel)
        g = jnp.where(pick, -jnp.inf, g)
    return jnp.where(lane < n_past, sel, 0.0)


def _block_selected(sel, n):
    lane = lax.broadcasted_iota(jnp.int32, sel.shape, 1)
    return jnp.max(jnp.where(lane == n, sel, 0.0), axis=1, keepdims=True) > 0.5


def _moba_prompt_body(far_ref, qf_ref, kt_ref, vt_ref, km_ref, bias_ref, g_ref, o_ref):
    pair = pl.program_id(1)
    c0 = pl.program_id(2)
    tq = MOBA_BLOCK
    qf = qf_ref[...]
    qb = (qf * Q_SCALE).astype(BF16)
    km = km_ref[0]
    row = lax.broadcasted_iota(jnp.int32, (tq, tq), 0)
    col = lax.broadcasted_iota(jnp.int32, (tq, tq), 1)
    out = jnp.zeros((tq, LANES), F32)
    for hh in range(PAIR):
        head = _head_lane_mask((tq, LANES), hh)
        far = far_ref[pair * PAIR + hh]
        gate = jnp.dot(jnp.where(head, qf, 0.0), km, precision=lax.Precision.HIGHEST,
                       preferred_element_type=F32)
        sel = _select_blocks(gate, c0)
        qm = jnp.where(head, qb, jnp.zeros_like(qb))

        s = _dot(qm, kt_ref[c0]) + bias_ref[hh, 0]
        s = jnp.where(col <= row, s, NEG_INF)
        m = jnp.max(s, axis=1, keepdims=True)
        p = jnp.exp(s - m)
        l = jnp.sum(p, axis=1, keepdims=True)
        acc = _dot_nt(p.astype(BF16), vt_ref[c0])

        def body(n, st, qm=qm, sel=sel, far=far, hh=hh):
            m, l, acc = st
            bias = jnp.where(n == c0 - 1, bias_ref[hh, 1], far)
            s = _dot(qm, kt_ref[n]) + bias
            s = jnp.where(_block_selected(sel, n), s, NEG_INF)
            m_new = jnp.maximum(m, jnp.max(s, axis=1, keepdims=True))
            alpha = jnp.exp(m - m_new)
            p = jnp.exp(s - m_new)
            l = alpha * l + jnp.sum(p, axis=1, keepdims=True)
            acc = alpha * acc + _dot_nt(p.astype(BF16), vt_ref[n])
            return m_new, l, acc

        m, l, acc = lax.fori_loop(0, c0, body, (m, l, acc))
        out = jnp.where(head, acc / l, out)
    o_ref[...] = (out * g_ref[...]).astype(BF16)


def _moba_prompt(far, qf, ktb, vtb, kmean, bias, gates, batch, seq):
    tq = MOBA_BLOCK
    nq = seq // tq
    tile, keys, gate_tile = _prompt_specs(nq, D_SB // LANES)
    return pl.pallas_call(
        _moba_prompt_body,
        grid=(batch, D_MB // LANES, nq),
        in_specs=[pl.BlockSpec(memory_space=pltpu.SMEM),
                  tile, keys, keys,
                  pl.BlockSpec((1, LANES, LANES), lambda b, p, i: (b, p, 0)),
                  pl.BlockSpec((PAIR, 2, tq, tq), lambda b, p, i: (p, 0, 0, 0)),
                  gate_tile],
        out_specs=tile,
        out_shape=jax.ShapeDtypeStruct((batch * seq, D_MB), BF16),
        compiler_params=_cparams(3),
        name="moba_prompt",
    )(far, qf, ktb, vtb, kmean, bias, gates)


def _mem_prompt_body(q_ref, mkt_ref, mvt_ref, g_ref, o_ref):
    tq = q_ref.shape[0]
    for pr in range(D_MEM // LANES):
        chans = slice(pr * LANES, (pr + 1) * LANES)
        q = q_ref[:, chans]
        mkt = mkt_ref[0, chans, :]
        mvt = mvt_ref[0, chans, :]
        out = jnp.zeros((tq, LANES), F32)
        for hh in range(PAIR):
            head = _head_lane_mask((tq, LANES), hh)
            s = _dot(jnp.where(head, q, jnp.zeros_like(q)), mkt)
            p = jnp.exp(s - jnp.max(s, axis=1, keepdims=True))
            o = _dot_nt(p.astype(BF16), mvt) / jnp.sum(p, axis=1, keepdims=True)
            out = jnp.where(head, o, out)
        o_ref[:, chans] = (out * g_ref[:, chans]).astype(BF16)


def _mem_prompt(qb, mktb, mvtb, gates, batch, seq, tq):
    nq = seq // tq
    n_mem = mktb.shape[2]
    g_off = (D_SB + D_MB) // D_MEM
    mem = pl.BlockSpec((1, D_MEM, n_mem), lambda b, i: (b, 0, 0))
    return pl.pallas_call(
        _mem_prompt_body,
        grid=(batch, nq),
        in_specs=[pl.BlockSpec((tq, D_MEM), lambda b, i: (b * nq + i, 0)), mem, mem,
                  pl.BlockSpec((tq, D_MEM), lambda b, i: (b * nq + i, g_off))],
        out_specs=pl.BlockSpec((tq, D_MEM), lambda b, i: (b * nq + i, 0)),
        out_shape=jax.ShapeDtypeStruct((batch * seq, D_MEM), BF16),
        compiler_params=_cparams(2),
        name="mem_prompt",
    )(qb, mktb, mvtb, gates)


def _head_rows(x_row, width):
    r = lax.broadcasted_iota(jnp.int32, (SUBLANES, width), 0)
    lane = lax.broadcasted_iota(jnp.int32, (SUBLANES, width), 1)
    keep = (lane >= r * HEAD_DIM) & (lane < (r + 1) * HEAD_DIM)
    return jnp.where(keep, jnp.broadcast_to(x_row.astype(F32), (SUBLANES, width)), 0.0), keep


def _collapse_heads(x, keep):
    return jnp.sum(jnp.where(keep, x, 0.0), axis=0, keepdims=True)


def _sb_sample_body(pt_ref, q_ref, g_ref, u_ref, *refs, n_pages):
    del pt_ref
    k_refs, v_refs, o_ref = refs[:n_pages], refs[n_pages:2 * n_pages], refs[2 * n_pages]
    qm, keep = _head_rows(q_ref[0], D_SB)
    qm = qm.astype(BF16)
    carry = jnp.zeros((SUBLANES, 1), F32)
    acc = jnp.zeros((SUBLANES, D_SB), F32)
    for p in reversed(range(n_pages)):
        z = _dot(qm, k_refs[p][0, 0].astype(BF16))
        log_keep, log_beta = _softplus_parts(z)
        suffix = _split_dot(log_keep, u_ref[...])
        a = jnp.exp(log_beta + suffix + carry)
        acc = acc + _dot_nt(a.astype(BF16), v_refs[p][0, 0].astype(BF16))
        carry = carry + suffix[:, 0:1] + log_keep[:, 0:1]
    o_ref[0] = (_collapse_heads(acc, keep) * g_ref[0]).astype(BF16)


def _page_specs(layer, n_pages, width):
    return [pl.BlockSpec((1, 1, width, PAGE_SIZE),
                         functools.partial(lambda b, pt, p: (layer, pt[b, p], 0, 0), p=p))
            for p in range(n_pages)]


def _row_spec(width, off=0):
    return pl.BlockSpec((1, 1, width), lambda b, pt: (b, 0, off))


def _sb_sample(layer, page_table, qb, gates3, u, cache_kt, cache_vt):
    db, n_pages = page_table.shape
    grid_spec = pltpu.PrefetchScalarGridSpec(
        num_scalar_prefetch=1,
        grid=(db,),
        in_specs=[_row_spec(D_SB), _row_spec(D_SB),
                  pl.BlockSpec((PAGE_SIZE, PAGE_SIZE), lambda b, pt: (0, 0))]
                 + _page_specs(layer, n_pages, D_SB) + _page_specs(layer, n_pages, D_SB),
        out_specs=_row_spec(D_SB),
    )
    return pl.pallas_call(
        functools.partial(_sb_sample_body, n_pages=n_pages),
        grid_spec=grid_spec,
        out_shape=jax.ShapeDtypeStruct((db, 1, D_SB), BF16),
        compiler_params=_cparams(1),
        name="sb_sample",
    )(page_table, qb.reshape(db, 1, D_SB), gates3, u, *([cache_kt] * n_pages), *([cache_vt] * n_pages))


def _moba_sample_body(pt_ref, qf_ref, kn_ref, vn_ref, bias_ref, bown_ref, g_ref, *refs, n_pages):
    del pt_ref
    k_refs, v_refs, o_ref = refs[:n_pages], refs[n_pages:2 * n_pages], refs[2 * n_pages]
    pages_per_block = MOBA_BLOCK // PAGE_SIZE
    n_past = n_pages // pages_per_block
    qf, keep = _head_rows(qf_ref[0], D_MB)
    q_hi, q_lo = _split_bf16(qf * Q_SCALE)

    scores = []
    for p in range(n_pages):
        kt = k_refs[p][0, 0].astype(BF16)
        scores.append(_dot(q_hi, kt) + _dot(q_lo, kt))
    lane = lax.broadcasted_iota(jnp.int32, (SUBLANES, LANES), 1)
    gate = jnp.zeros((SUBLANES, LANES), F32)
    for n in range(n_past):
        total = sum(jnp.sum(scores[p], axis=1, keepdims=True)
                    for p in range(n * pages_per_block, (n + 1) * pages_per_block))
        gate = jnp.where(lane == n, total * (1.0 / (Q_SCALE * MOBA_BLOCK)), gate)
    sel = _select_blocks(gate, n_past)

    kn = kn_ref[0].astype(BF16).astype(F32)
    s_own = jnp.sum(qf * Q_SCALE * kn, axis=1, keepdims=True) + bown_ref[:, 0:1]
    m = s_own
    for p in range(n_pages):
        s = scores[p] + bias_ref[:, p * PAGE_SIZE:(p + 1) * PAGE_SIZE]
        s = jnp.where(_block_selected(sel, p // pages_per_block), s, NEG_INF)
        scores[p] = s
        m = jnp.maximum(m, jnp.max(s, axis=1, keepdims=True))
    p_own = jnp.exp(s_own - m)
    l = p_own
    acc = p_own * vn_ref[0]
    for p in range(n_pages):
        w = jnp.exp(scores[p] - m)
        l = l + jnp.sum(w, axis=1, keepdims=True)
        acc = acc + _dot_nt(w.astype(BF16), v_refs[p][0, 0].astype(BF16))
    o_ref[0] = (_collapse_heads(acc / l, keep) * g_ref[0]).astype(BF16)


def _moba_sample(layer, page_table, qf, k_new, v_new, bias_keys, bias_own, gates3, cache_kt, cache_vt):
    db, n_pages = page_table.shape
    past = n_pages * PAGE_SIZE
    grid_spec = pltpu.PrefetchScalarGridSpec(
        num_scalar_prefetch=1,
        grid=(db,),
        in_specs=[_row_spec(D_MB), _row_spec(D_MB), _row_spec(D_MB),
                  pl.BlockSpec((SUBLANES, past), lambda b, pt: (0, 0)),
                  pl.BlockSpec((SUBLANES, LANES), lambda b, pt: (0, 0)),
                  _row_spec(D_MB, D_SB // D_MB)]
                 + _page_specs(layer, n_pages, D_MB) + _page_specs(layer, n_pages, D_MB),
        out_specs=_row_spec(D_MB),
    )
    r3 = lambda a: a.reshape(db, 1, D_MB)
    return pl.pallas_call(
        functools.partial(_moba_sample_body, n_pages=n_pages),
        grid_spec=grid_spec,
        out_shape=jax.ShapeDtypeStruct((db, 1, D_MB), BF16),
        compiler_params=_cparams(1),
        name="moba_sample",
    )(page_table, r3(qf), r3(k_new), r3(v_new), bias_keys, bias_own, gates3,
      *([cache_kt] * n_pages), *([cache_vt] * n_pages))


def _mem_sample_body(q_ref, mkt_ref, mvt_ref, g_ref, o_ref):
    qm, keep = _head_rows(q_ref[0], D_MEM)
    qm = qm.astype(BF16)
    s = _dot(qm, mkt_ref[0, 0].astype(BF16))
    p = jnp.exp(s - jnp.max(s, axis=1, keepdims=True))
    o = _dot_nt(p.astype(BF16), mvt_ref[0, 0].astype(BF16)) / jnp.sum(p, axis=1, keepdims=True)
    o_ref[0] = (_collapse_heads(o, keep) * g_ref[0]).astype(BF16)


def _mem_sample(layer, qb, gates3, mem_kt, mem_vt):
    db, n_mem = mem_kt.shape[1], mem_kt.shape[3]
    mem = pl.BlockSpec((1, 1, D_MEM, n_mem), lambda b: (layer, b, 0, 0))
    row = lambda off: pl.BlockSpec((1, 1, D_MEM), lambda b: (b, 0, off))
    return pl.pallas_call(
        _mem_sample_body,
        grid=(db,),
        in_specs=[row(0), mem, mem, row((D_SB + D_MB) // D_MEM)],
        out_specs=row(0),
        out_shape=jax.ShapeDtypeStruct((db, 1, D_MEM), BF16),
        compiler_params=_cparams(1),
        name="mem_sample",
    )(qb.reshape(db, 1, D_MEM), mem_kt, mem_vt, gates3)


def _upper_ones(n):
    i = np.arange(n)
    return jnp.asarray((i[:, None] > i[None, :]).astype(np.float32), dtype=BF16)


def _channel_major(cache, heads):
    lead = cache.shape[:-3]
    t = jnp.moveaxis(cache, -3, -1)
    return t.reshape(lead + (heads * HEAD_DIM, cache.shape[-3]))


def _token_major(x, heads):
    lead = x.shape[:-2]
    return jnp.moveaxis(x.reshape(lead + (heads, HEAD_DIM, x.shape[-1])), -1, -3)


def kernel(x_prompt, x_sample, cache_sb_k, cache_sb_v, cache_moba_k, cache_moba_v, cache_mem_k,
           cache_mem_v, page_table, mem_prompt, rel_bias, g_pre, g_post, g_mem, w_in, w_out, w_mem_kv):
    batch, seq, d_model = x_prompt.shape
    db, dec_seq, _ = x_sample.shape
    depth = w_in.shape[0]
    n_mem = mem_prompt.shape[1]
    n_pages = page_table.shape[1]
    past = n_pages * PAGE_SIZE
    assert dec_seq == 1 and seq % TOKEN_TILE == 0 and past % MOBA_BLOCK == 0
    assert past >= 2 * MOBA_BLOCK

    tm = TOKEN_TILE
    tiles = seq // tm
    u_block = _upper_ones(TOKEN_TILE)
    u_page = _upper_ones(PAGE_SIZE)
    p_outs, p_plan = _prompt_proj_plan()
    s_outs, s_plan = _sample_proj_plan()
    kv_outs, kv_plan = _mem_kv_plan()

    bias = _bias_tiles(rel_bias)
    far = rel_bias[NUM_BUCKETS - 1]
    pad = jnp.zeros((SUBLANES - H_MB, past), F32)
    bias_keys = jnp.concatenate(
        [jnp.concatenate([jnp.broadcast_to(far[:, None], (H_MB, past - MOBA_BLOCK)), bias[:, 1, 0, :]], axis=1),
         pad], axis=0)
    bias_own = jnp.concatenate(
        [jnp.broadcast_to(bias[:, 0, 0, 0:1], (H_MB, LANES)), pad[:, :LANES]], axis=0)

    sb_kt, sb_vt = _channel_major(cache_sb_k, H_SB), _channel_major(cache_sb_v, H_SB)
    mb_kt, mb_vt = _channel_major(cache_moba_k, H_MB), _channel_major(cache_moba_v, H_MB)
    mem_kt, mem_vt = _channel_major(cache_mem_k, H_MEM), _channel_major(cache_mem_v, H_MEM)

    xp = x_prompt.reshape(batch * seq, d_model)
    xs = x_sample.reshape(db, d_model)
    mem2 = mem_prompt.reshape(batch * n_mem, d_model)
    pt = page_table.astype(jnp.int32)
    outs = [[] for _ in range(10)]
    for l in range(depth):
        w_in_l = w_in[l].astype(BF16)
        w_kv_t = _kv_weight_t(w_in[l])
        w_out_l = w_out[l].astype(BF16)

        (kt_sb, vt_sb, kt_mb, vt_mb, ktb_sb, vtb_sb, ktb_mb, vtb_mb, qb_sb, qf_mb, qb_mem, gates) = _norm_proj(
            xp, g_pre[l], w_in_l, w_kv_t, p_outs, p_plan, tm, tiles)
        mkt, mvt, mktb, mvtb = _norm_proj(mem2, g_mem[l], None, w_mem_kv[l].T.astype(BF16),
                                          kv_outs, kv_plan, n_mem, 1)
        m_sb = _sb_prompt(qb_sb, ktb_sb, vtb_sb, gates, u_block, batch, seq)
        kmean = _kmean(kt_mb, batch, seq)
        m_mb = _moba_prompt(far, qf_mb, ktb_mb, vtb_mb, kmean, bias, gates, batch, seq)
        m_mem = _mem_prompt(qb_mem, mktb, mvtb, gates, batch, seq, 2 * tm)
        xp = _merge(m_sb, m_mb, m_mem, xp, w_out_l, g_post[l], 2 * tm)
        for i, (a, h) in enumerate(((kt_sb, H_SB), (vt_sb, H_SB), (kt_mb, H_MB), (vt_mb, H_MB),
                                    (mkt, H_MEM), (mvt, H_MEM))):
            outs[i].append(_token_major(a, h))

        (kt_sb, vt_sb, kt_mb, vt_mb, k_mb, v_mb, qb_sb, qf_mb, qb_mem, gates) = _norm_proj(
            xs, g_pre[l], w_in_l, w_kv_t, s_outs, s_plan, db, 1)
        gates3 = gates.reshape(db, 1, D_MIX)
        s_sb = _sb_sample(l, pt, qb_sb, gates3, u_page, sb_kt, sb_vt)
        s_mb = _moba_sample(l, pt, qf_mb, k_mb, v_mb, bias_keys, bias_own, gates3, mb_kt, mb_vt)
        s_mem = _mem_sample(l, qb_mem, gates3, mem_kt, mem_vt)
        xs = _merge(s_sb.reshape(db, D_SB), s_mb.reshape(db, D_MB), s_mem.reshape(db, D_MEM),
                    xs, w_out_l, g_post[l], db)
        for i, (a, h) in enumerate(((kt_sb, H_SB), (vt_sb, H_SB), (kt_mb, H_MB), (vt_mb, H_MB))):
            outs[6 + i].append(_token_major(a, h).reshape(db, 1, h, HEAD_DIM))

    return (xp.reshape(batch, seq, d_model), xs.reshape(db, 1, d_model)) + tuple(jnp.stack(o) for o in outs)
```

```python
import functools
import math

import numpy as np
import jax
import jax.numpy as jnp
from jax import lax
from jax.experimental import pallas as pl
from jax.experimental.pallas import tpu as pltpu

HEAD_DIM = 64
H_SB, H_MB, H_MEM = 6, 6, 4
D_SB, D_MB, D_MEM = H_SB * HEAD_DIM, H_MB * HEAD_DIM, H_MEM * HEAD_DIM
D_MIX = D_SB + D_MB + D_MEM
MOBA_BLOCK = 256
MOBA_TOPK = 3
PAGE_SIZE = 128
NUM_BUCKETS = 32
MAX_EXACT = 16
MAX_DISTANCE = 128
RMS_EPS = 1e-6
NEG_INF = -1e30
Q_SCALE = 1.0 / math.sqrt(HEAD_DIM)

LANES = 128
SUBLANES = 8
PAIR = LANES // HEAD_DIM
VMEM_LIMIT = 48 * 1024 * 1024
TOKEN_TILE = MOBA_BLOCK

SB_EXIT = 110.0
SB_SAMPLE_HEAD_PAGES = 2

BF16 = jnp.bfloat16
F32 = jnp.float32
_NT = (((1,), (1,)), ((), ()))


def _cparams(n_axes):
    return pltpu.CompilerParams(dimension_semantics=("arbitrary",) * n_axes,
                                vmem_limit_bytes=VMEM_LIMIT)


def _dot(a, b):
    return jnp.dot(a, b, preferred_element_type=F32)


def _dot_nt(a, b):
    return lax.dot_general(a, b, _NT, preferred_element_type=F32)


def _split_bf16(x):
    hi = x.astype(BF16)
    return hi, (x - hi.astype(F32)).astype(BF16)


def _split_dot(x, u):
    hi, lo = _split_bf16(x)
    return _dot(hi, u) + _dot(lo, u)


def _head_lane_mask(shape, hh):
    lane = lax.broadcasted_iota(jnp.int32, shape, len(shape) - 1)
    return (lane >= HEAD_DIM) if hh else (lane < HEAD_DIM)


def _softplus_parts(z):
    t = jnp.log(1.0 + jnp.exp(-jnp.abs(z)))
    return jnp.minimum(-z, 0.0) - t, jnp.minimum(z, 0.0) - t


def _sb_block(qm, kt, vt, u, carry, acc, mask=None):
    z = _dot(qm, kt)
    log_keep, log_beta = _softplus_parts(z)
    if mask is not None:
        log_keep = jnp.where(mask, log_keep, 0.0)
    suffix = _split_dot(log_keep, u)
    a = jnp.exp(log_beta + suffix + carry)
    if mask is not None:
        a = jnp.where(mask, a, 0.0)
    acc = acc + _dot_nt(a.astype(BF16), vt)
    carry = carry + suffix[:, 0:1] + log_keep[:, 0:1]
    return carry, acc


def _norm_proj_body(*refs, plan, has_w, has_wt):
    x_ref, g_ref = refs[:2]
    rest = list(refs[2:])
    w_ref = rest.pop(0) if has_w else None
    wt_ref = rest.pop(0) if has_wt else None
    out_refs = rest
    x = x_ref[...]
    ms = jnp.mean(x * x, axis=-1, keepdims=True)
    h = (x * lax.rsqrt(ms + RMS_EPS) * g_ref[...]).astype(BF16)
    for (transposed, lo, hi), writes in plan:
        p = _dot_nt(wt_ref[lo:hi, :], h) if transposed else _dot(h, w_ref[:, lo:hi])
        for oi, dlo, kind in writes:
            if kind == "f32":
                val = p
            elif kind == "bf16":
                val = p.astype(BF16)
            elif kind == "bf16_scaled":
                val = (p * Q_SCALE).astype(BF16)
            else:
                val = p * (1.0 / (1.0 + jnp.exp(-p)))
            if transposed:
                out_refs[oi][0] = val
            else:
                out_refs[oi][:, dlo:dlo + (hi - lo)] = val


def _norm_proj(x2, gain, w, wt, outs, plan, tm, tiles_per_batch):
    n, d = x2.shape
    nt = n // tm
    in_specs = [pl.BlockSpec((tm, d), lambda i: (i, 0)), pl.BlockSpec((1, d), lambda i: (0, 0))]
    args = [x2, gain.reshape(1, d)]
    for m in (w, wt):
        if m is not None:
            in_specs.append(pl.BlockSpec(m.shape, lambda i: (0, 0)))
            args.append(m)
    out_specs, out_shape = [], []
    for kind, width, dt in outs:
        if kind == "rows":
            out_specs.append(pl.BlockSpec((tm, width), lambda i: (i, 0)))
            out_shape.append(jax.ShapeDtypeStruct((n, width), dt))
        elif kind == "t_batch":
            out_specs.append(pl.BlockSpec((1, width, tm),
                                          lambda i: (i // tiles_per_batch, 0, i % tiles_per_batch)))
            out_shape.append(jax.ShapeDtypeStruct((nt // tiles_per_batch, width, tiles_per_batch * tm), dt))
        else:
            out_specs.append(pl.BlockSpec((1, width, tm), lambda i: (i, 0, 0)))
            out_shape.append(jax.ShapeDtypeStruct((nt, width, tm), dt))
    return pl.pallas_call(
        functools.partial(_norm_proj_body, plan=plan, has_w=w is not None, has_wt=wt is not None),
        grid=(nt,),
        in_specs=in_specs,
        out_specs=out_specs,
        out_shape=out_shape,
        compiler_params=_cparams(1),
        name="norm_proj",
    )(*args)


_SEG = {}
_c = 0
for _name, _w in (("q_sb", D_SB), ("k_sb", D_SB), ("v_sb", D_SB), ("z_sb", D_SB),
                  ("q_mb", D_MB), ("k_mb", D_MB), ("v_mb", D_MB), ("z_mb", D_MB),
                  ("q_mem", D_MEM), ("z_mem", D_MEM)):
    _SEG[_name] = (_c, _c + _w)
    _c += _w
_KV_NAMES = ("k_sb", "v_sb", "k_mb", "v_mb")


def _kv_weight_t(w_in_l):
    return jnp.concatenate([w_in_l[:, _SEG[s][0]:_SEG[s][1]] for s in _KV_NAMES], axis=1).T.astype(BF16)


def _prompt_proj_plan():
    outs = [("t_batch", D_SB, F32)] * 4 + [("t_tile", D_SB, BF16)] * 4
    outs += [("rows", D_SB, BF16), ("rows", D_MB, F32), ("rows", D_MEM, BF16),
             ("rows", D_MIX, F32)]
    plan = [((True, i * D_SB, (i + 1) * D_SB), ((i, 0, "f32"), (4 + i, 0, "bf16"))) for i in range(4)]
    plan += [((False,) + _SEG["q_sb"], ((8, 0, "bf16_scaled"),)),
             ((False,) + _SEG["q_mb"], ((9, 0, "f32"),)),
             ((False,) + _SEG["q_mem"], ((10, 0, "bf16_scaled"),)),
             ((False,) + _SEG["z_sb"], ((11, 0, "silu"),)),
             ((False,) + _SEG["z_mb"], ((11, D_SB, "silu"),)),
             ((False,) + _SEG["z_mem"], ((11, D_SB + D_MB, "silu"),))]
    return outs, tuple(plan)


def _sample_proj_plan():
    outs = [("t_batch", D_SB, F32)] * 4
    outs += [("rows", D_MB, F32), ("rows", D_MB, F32)]
    outs += [("rows", D_SB, BF16), ("rows", D_MB, F32), ("rows", D_MEM, BF16),
             ("rows", D_MIX, F32)]
    plan = [((True, i * D_SB, (i + 1) * D_SB), ((i, 0, "f32"),)) for i in range(4)]
    plan += [((False,) + _SEG["k_mb"], ((4, 0, "f32"),)),
             ((False,) + _SEG["v_mb"], ((5, 0, "f32"),)),
             ((False,) + _SEG["q_sb"], ((6, 0, "bf16_scaled"),)),
             ((False,) + _SEG["q_mb"], ((7, 0, "f32"),)),
             ((False,) + _SEG["q_mem"], ((8, 0, "bf16_scaled"),)),
             ((False,) + _SEG["z_sb"], ((9, 0, "silu"),)),
             ((False,) + _SEG["z_mb"], ((9, D_SB, "silu"),)),
             ((False,) + _SEG["z_mem"], ((9, D_SB + D_MB, "silu"),))]
    return outs, tuple(plan)


def _mem_kv_plan():
    outs = [("t_tile", D_MEM, F32), ("t_tile", D_MEM, F32), ("t_tile", D_MEM, BF16), ("t_tile", D_MEM, BF16)]
    plan = (((True, 0, D_MEM), ((0, 0, "f32"), (2, 0, "bf16"))),
            ((True, D_MEM, 2 * D_MEM), ((1, 0, "f32"), (3, 0, "bf16"))))
    return outs, plan


def _merge_body(msb_ref, mmb_ref, mmem_ref, x_ref, w_ref, g_ref, o_ref):
    y = (_dot(msb_ref[...], w_ref[0:D_SB, :])
         + _dot(mmb_ref[...], w_ref[D_SB:D_SB + D_MB, :])
         + _dot(mmem_ref[...], w_ref[D_SB + D_MB:D_MIX, :]))
    ms = jnp.mean(y * y, axis=-1, keepdims=True)
    o_ref[...] = x_ref[...] + y * lax.rsqrt(ms + RMS_EPS) * g_ref[...]


def _merge(msb, mmb, mmem, x2, w_out_bf16, g_post, tm):
    n, d = x2.shape
    row = lambda w: pl.BlockSpec((tm, w), lambda i: (i, 0))
    return pl.pallas_call(
        _merge_body,
        grid=(n // tm,),
        in_specs=[row(D_SB), row(D_MB), row(D_MEM), row(d),
                  pl.BlockSpec((D_MIX, d), lambda i: (0, 0)),
                  pl.BlockSpec((1, d), lambda i: (0, 0))],
        out_specs=row(d),
        out_shape=jax.ShapeDtypeStruct((n, d), F32),
        compiler_params=_cparams(1),
        name="merge",
    )(msb, mmb, mmem, x2, w_out_bf16, g_post.reshape(1, d))


def _sb_prompt_body(q_ref, kt_ref, vt_ref, u_ref, g_ref, o_ref):
    tq = TOKEN_TILE
    qi = pl.program_id(2)
    q = q_ref[...]
    row = lax.broadcasted_iota(jnp.int32, (tq, tq), 0)
    col = lax.broadcasted_iota(jnp.int32, (tq, tq), 1)
    heads = [_head_lane_mask((tq, LANES), hh) for hh in range(PAIR)]
    qms = [jnp.where(head, q, jnp.zeros_like(q)) for head in heads]

    state = []
    for qm in qms:
        state += _sb_block(qm, kt_ref[qi], vt_ref[qi], u_ref[...], jnp.zeros((tq, 1), F32),
                           jnp.zeros((tq, LANES), F32), mask=col < row)

    def carry_max(st):
        return functools.reduce(jnp.maximum, [jnp.max(c) for c in st[0::2]])

    def cond(loop):
        j, cmax = loop[:2]
        return jnp.logical_and(j >= 0, cmax > -SB_EXIT)

    def body(loop):
        j, st = loop[0], loop[2:]
        new = []
        for hh, qm in enumerate(qms):
            new += _sb_block(qm, kt_ref[j], vt_ref[j], u_ref[...], st[2 * hh], st[2 * hh + 1])
        return (j - 1, carry_max(new)) + tuple(new)

    final = lax.while_loop(cond, body, (qi - 1, carry_max(state)) + tuple(state))[2:]
    out = jnp.where(heads[0], final[1], final[3])
    o_ref[...] = (out * g_ref[...]).astype(BF16)


def _prompt_specs(nq, g_off):
    tq = TOKEN_TILE
    tile = lambda off: pl.BlockSpec((tq, LANES), lambda b, p, i: (b * nq + i, p + off))
    keys = pl.BlockSpec((nq, LANES, tq), lambda b, p, i: (b, p, 0))
    return tile(0), keys, tile(g_off)


def _sb_prompt(qb, ktb, vtb, gates, u, batch, seq):
    tq = TOKEN_TILE
    nq = seq // tq
    tile, keys, gate_tile = _prompt_specs(nq, 0)
    return pl.pallas_call(
        _sb_prompt_body,
        grid=(batch, D_SB // LANES, nq),
        in_specs=[tile, keys, keys, pl.BlockSpec((tq, tq), lambda b, p, i: (0, 0)), gate_tile],
        out_specs=tile,
        out_shape=jax.ShapeDtypeStruct((batch * seq, D_SB), BF16),
        compiler_params=_cparams(3),
        name="sb_prompt",
    )(qb, ktb, vtb, u, gates)


def _kmean_body(kt_ref, o_ref):
    nb = pl.program_id(1)
    mean = jnp.sum(kt_ref[0], axis=1, keepdims=True) * (1.0 / MOBA_BLOCK)
    lane = lax.broadcasted_iota(jnp.int32, o_ref.shape[1:], 1)

    @pl.when(nb == 0)
    def _():
        o_ref[0] = jnp.zeros(o_ref.shape[1:], F32)

    o_ref[0] = jnp.where(lane == nb, mean, o_ref[0])


def _kmean(kt_f32, batch, seq):
    nb = seq // MOBA_BLOCK
    assert nb <= LANES
    return pl.pallas_call(
        _kmean_body,
        grid=(batch, nb),
        in_specs=[pl.BlockSpec((1, D_MB, MOBA_BLOCK), lambda b, n: (b, 0, n))],
        out_specs=pl.BlockSpec((1, D_MB, LANES), lambda b, n: (b, 0, 0)),
        out_shape=jax.ShapeDtypeStruct((batch, D_MB, LANES), F32),
        compiler_params=_cparams(2),
        name="kmean",
    )(kt_f32)


def _t5_bucket_np(dist):
    n = np.maximum(dist, 0)
    nf = np.maximum(n, 1).astype(np.float32)
    large = MAX_EXACT + (np.log(nf / np.float32(MAX_EXACT)) / np.float32(math.log(MAX_DISTANCE / MAX_EXACT))
                         * np.float32(NUM_BUCKETS - MAX_EXACT)).astype(np.int32)
    large = np.minimum(large, NUM_BUCKETS - 1)
    return np.where(n < MAX_EXACT, n, large).astype(np.int32)


def _bias_tiles_body(tab_ref, bucket_ref, o_ref):
    h = pl.program_id(0)
    for t in range(2):
        bucket = bucket_ref[t]
        acc = jnp.zeros(bucket.shape, F32)
        for b in range(NUM_BUCKETS):
            acc = jnp.where(bucket == b, tab_ref[b, h], acc)
        o_ref[0, t] = acc - tab_ref[NUM_BUCKETS - 1, h]


def _bias_tiles(rel_bias):
    assert MOBA_BLOCK >= MAX_DISTANCE
    i = np.arange(MOBA_BLOCK)[:, None]
    j = np.arange(MOBA_BLOCK)[None, :]
    buckets = np.stack([_t5_bucket_np(i - j), _t5_bucket_np(MOBA_BLOCK + i - j)])
    return pl.pallas_call(
        _bias_tiles_body,
        grid=(H_MB,),
        in_specs=[pl.BlockSpec(memory_space=pltpu.SMEM),
                  pl.BlockSpec((2, MOBA_BLOCK, MOBA_BLOCK), lambda h: (0, 0, 0))],
        out_specs=pl.BlockSpec((1, 2, MOBA_BLOCK, MOBA_BLOCK), lambda h: (h, 0, 0, 0)),
        out_shape=jax.ShapeDtypeStruct((H_MB, 2, MOBA_BLOCK, MOBA_BLOCK), F32),
        compiler_params=_cparams(1),
        name="bias_tiles",
    )(rel_bias, jnp.asarray(buckets))


def _select_blocks(gate, n_past):
    lane = lax.broadcasted_iota(jnp.int32, gate.shape, 1)
    lane_f = lane.astype(F32)
    g = jnp.where(lane < n_past, gate, NEG_INF)
    sel = jnp.zeros(gate.shape, F32)
    for _ in range(MOBA_TOPK):
        m = jnp.max(g, axis=1, keepdims=True)
        first = jnp.min(jnp.where(g == m, lane_f, float(LANES)), axis=1, keepdims=True)
        pick = lane_f == first
        sel = jnp.where(pick, 1.0, sel)
        g = jnp.where(pick, -jnp.inf, g)
    return jnp.where(lane < n_past, sel, 0.0)


def _block_selected(sel, n):
    lane = lax.broadcasted_iota(jnp.int32, sel.shape, 1)
    return jnp.max(jnp.where(lane == n, sel, 0.0), axis=1, keepdims=True) > 0.5


def _moba_prompt_body(qf_ref, kt_ref, vt_ref, km_ref, bias_ref, e_ref, g_ref, o_ref):
    c0 = pl.program_id(2)
    tq = MOBA_BLOCK
    qf = qf_ref[...]
    qb = (qf * Q_SCALE).astype(BF16)
    km = km_ref[0]
    row = lax.broadcasted_iota(jnp.int32, (tq, tq), 0)
    col = lax.broadcasted_iota(jnp.int32, (tq, tq), 1)
    ones = jnp.ones((LANES, tq), BF16)
    heads = [_head_lane_mask((tq, LANES), hh) for hh in range(PAIR)]

    def with_ones(n):
        return jnp.concatenate([vt_ref[n], ones], axis=0)

    def keys_with_pick(n):
        return jnp.concatenate([kt_ref[n], e_ref[n]], axis=0)

    def update(st, s, vals):
        m, l, acc = st
        m_new = jnp.maximum(m, jnp.max(s, axis=1, keepdims=True))
        alpha = jnp.exp(m - m_new)
        pv = _dot_nt(jnp.exp(s - m_new).astype(BF16), vals)
        return m_new, alpha * l + pv[:, LANES:LANES + 1], alpha * acc + pv[:, :LANES]

    prev = jnp.maximum(c0 - 1, 0)
    lhs, state = [], []
    for hh, head in enumerate(heads):
        gate = jnp.dot(jnp.where(head, qf, 0.0), km, precision=lax.Precision.HIGHEST,
                       preferred_element_type=F32)
        penalty = jnp.where(_select_blocks(gate, c0) > 0.5, 0.0, NEG_INF).astype(BF16)
        qm = jnp.where(head, qb, jnp.zeros_like(qb))
        lhs.append(jnp.concatenate([qm, penalty], axis=1))

        s = jnp.where(col <= row, _dot(qm, kt_ref[c0]) + bias_ref[hh, 0], NEG_INF)
        m = jnp.max(s, axis=1, keepdims=True)
        pv = _dot_nt(jnp.exp(s - m).astype(BF16), with_ones(c0))
        st = (m, pv[:, LANES:LANES + 1], pv[:, :LANES])
        s = _dot(lhs[hh], keys_with_pick(prev)) + bias_ref[hh, 1]
        state += update(st, s, with_ones(prev))

    def body(n, st):
        keys, vals = keys_with_pick(n), with_ones(n)
        new = []
        for hh in range(PAIR):
            new += update(st[3 * hh:3 * hh + 3], _dot(lhs[hh], keys), vals)
        return tuple(new)

    final = lax.fori_loop(0, prev, body, tuple(state))
    out = jnp.where(heads[0], final[2] / final[1], final[5] / final[4])
    o_ref[...] = (out * g_ref[...]).astype(BF16)


def _pick_rows(n_blocks):
    n = np.arange(n_blocks)[:, None, None]
    r = np.arange(LANES)[None, :, None]
    return jnp.broadcast_to(jnp.asarray(n == r), (n_blocks, LANES, MOBA_BLOCK)).astype(BF16)


def _moba_prompt(qf, ktb, vtb, kmean, bias, gates, batch, seq):
    tq = MOBA_BLOCK
    nq = seq // tq
    tile, keys, gate_tile = _prompt_specs(nq, D_SB // LANES)
    return pl.pallas_call(
        _moba_prompt_body,
        grid=(batch, D_MB // LANES, nq),
        in_specs=[tile, keys, keys,
                  pl.BlockSpec((1, LANES, LANES), lambda b, p, i: (b, p, 0)),
                  pl.BlockSpec((PAIR, 2, tq, tq), lambda b, p, i: (p, 0, 0, 0)),
                  pl.BlockSpec((nq, LANES, tq), lambda b, p, i: (0, 0, 0)),
                  gate_tile],
        out_specs=tile,
        out_shape=jax.ShapeDtypeStruct((batch * seq, D_MB), BF16),
        compiler_params=_cparams(3),
        name="moba_prompt",
    )(qf, ktb, vtb, kmean, bias, _pick_rows(nq), gates)


def _mem_prompt_body(q_ref, mkt_ref, mvt_ref, g_ref, o_ref):
    tq = q_ref.shape[0]
    for pr in range(D_MEM // LANES):
        chans = slice(pr * LANES, (pr + 1) * LANES)
        q = q_ref[:, chans]
        mkt = mkt_ref[0, chans, :]
        mvt = mvt_ref[0, chans, :]
        out = jnp.zeros((tq, LANES), F32)
        for hh in range(PAIR):
            head = _head_lane_mask((tq, LANES), hh)
            s = _dot(jnp.where(head, q, jnp.zeros_like(q)), mkt)
            p = jnp.exp(s - jnp.max(s, axis=1, keepdims=True))
            o = _dot_nt(p.astype(BF16), mvt) / jnp.sum(p, axis=1, keepdims=True)
            out = jnp.where(head, o, out)
        o_ref[:, chans] = (out * g_ref[:, chans]).astype(BF16)


def _mem_prompt(qb, mktb, mvtb, gates, batch, seq, tq):
    nq = seq // tq
    n_mem = mktb.shape[2]
    g_off = (D_SB + D_MB) // D_MEM
    mem = pl.BlockSpec((1, D_MEM, n_mem), lambda b, i: (b, 0, 0))
    return pl.pallas_call(
        _mem_prompt_body,
        grid=(batch, nq),
        in_specs=[pl.BlockSpec((tq, D_MEM), lambda b, i: (b * nq + i, 0)), mem, mem,
                  pl.BlockSpec((tq, D_MEM), lambda b, i: (b * nq + i, g_off))],
        out_specs=pl.BlockSpec((tq, D_MEM), lambda b, i: (b * nq + i, 0)),
        out_shape=jax.ShapeDtypeStruct((batch * seq, D_MEM), BF16),
        compiler_params=_cparams(2),
        name="mem_prompt",
    )(qb, mktb, mvtb, gates)


def _head_rows(x_row, width):
    r = lax.broadcasted_iota(jnp.int32, (SUBLANES, width), 0)
    lane = lax.broadcasted_iota(jnp.int32, (SUBLANES, width), 1)
    keep = (lane >= r * HEAD_DIM) & (lane < (r + 1) * HEAD_DIM)
    return jnp.where(keep, jnp.broadcast_to(x_row.astype(F32), (SUBLANES, width)), 0.0), keep


def _collapse_heads(x, keep):
    return jnp.sum(jnp.where(keep, x, 0.0), axis=0, keepdims=True)


def _sb_sample_chain(qm, u, k_refs, v_refs, carry, acc):
    for k_ref, v_ref in zip(reversed(k_refs), reversed(v_refs)):
        carry, acc = _sb_block(qm, k_ref[0, 0].astype(BF16), v_ref[0, 0].astype(BF16), u, carry, acc)
    return carry, acc


def _sb_sample_head_body(pt_ref, q_ref, u_ref, *refs, n_pages):
    del pt_ref
    k_refs, v_refs = refs[:n_pages], refs[n_pages:2 * n_pages]
    acc_ref, carry_ref, more_ref = refs[2 * n_pages:]
    qm, _ = _head_rows(q_ref[0], D_SB)
    carry, acc = _sb_sample_chain(qm.astype(BF16), u_ref[...], k_refs, v_refs,
                                  jnp.zeros((SUBLANES, 1), F32), jnp.zeros((SUBLANES, D_SB), F32))
    acc_ref[0] = acc
    carry_ref[0] = jnp.broadcast_to(carry, (SUBLANES, LANES))
    live = lax.broadcasted_iota(jnp.int32, (SUBLANES, 1), 0) < H_SB
    more = jnp.max(jnp.where(live, carry, -jnp.inf)) > -SB_EXIT
    more_ref[0] = jnp.broadcast_to(more.astype(jnp.int32), (SUBLANES, LANES))


def _sb_sample_tail_body(pt_ref, more_ref, q_ref, g_ref, u_ref, acc_ref, carry_ref, *refs, n_pages):
    del pt_ref
    b = pl.program_id(0)
    k_refs, v_refs, o_ref = refs[:n_pages], refs[n_pages:2 * n_pages], refs[2 * n_pages]
    qm, keep = _head_rows(q_ref[0], D_SB)
    acc0 = acc_ref[0]

    def rest():
        return _sb_sample_chain(qm.astype(BF16), u_ref[...], k_refs, v_refs, carry_ref[0][:, 0:1], acc0)[1]

    acc = lax.cond(more_ref[b] != 0, rest, lambda: acc0)
    o_ref[0] = (_collapse_heads(acc, keep) * g_ref[0]).astype(BF16)


def _page_specs(layer, pages, width, gated=False):
    def spec(p):
        if gated:
            return pl.BlockSpec((1, 1, width, PAGE_SIZE),
                                lambda b, pt, more: (layer, jnp.where(more[b] != 0, pt[b, p], 0), 0, 0))
        return pl.BlockSpec((1, 1, width, PAGE_SIZE), lambda b, pt: (layer, pt[b, p], 0, 0))
    return [spec(p) for p in pages]


def _row_spec(width, off=0):
    return pl.BlockSpec((1, 1, width), lambda b, *_: (b, 0, off))


def _sb_sample(layer, page_table, qb, gates3, u, cache_kt, cache_vt):
    db, n_pages = page_table.shape
    q3 = qb.reshape(db, 1, D_SB)
    head_pages = range(n_pages - SB_SAMPLE_HEAD_PAGES, n_pages)
    tail_pages = range(0, n_pages - SB_SAMPLE_HEAD_PAGES)
    u_spec = pl.BlockSpec((PAGE_SIZE, PAGE_SIZE), lambda b, *_: (0, 0))
    state = lambda w: pl.BlockSpec((1, SUBLANES, w), lambda b, *_: (b, 0, 0))
    acc, carry, more = pl.pallas_call(
        functools.partial(_sb_sample_head_body, n_pages=len(head_pages)),
        grid_spec=pltpu.PrefetchScalarGridSpec(
            num_scalar_prefetch=1,
            grid=(db,),
            in_specs=[_row_spec(D_SB), u_spec]
                     + _page_specs(layer, head_pages, D_SB) + _page_specs(layer, head_pages, D_SB),
            out_specs=[state(D_SB), state(LANES), state(LANES)],
        ),
        out_shape=[jax.ShapeDtypeStruct((db, SUBLANES, D_SB), F32),
                   jax.ShapeDtypeStruct((db, SUBLANES, LANES), F32),
                   jax.ShapeDtypeStruct((db, SUBLANES, LANES), jnp.int32)],
        compiler_params=_cparams(1),
        name="sb_sample_head",
    )(page_table, q3, u, *([cache_kt] * len(head_pages)), *([cache_vt] * len(head_pages)))
    return pl.pallas_call(
        functools.partial(_sb_sample_tail_body, n_pages=len(tail_pages)),
        grid_spec=pltpu.PrefetchScalarGridSpec(
            num_scalar_prefetch=2,
            grid=(db,),
            in_specs=[_row_spec(D_SB), _row_spec(D_SB), u_spec, state(D_SB), state(LANES)]
                     + _page_specs(layer, tail_pages, D_SB, gated=True)
                     + _page_specs(layer, tail_pages, D_SB, gated=True),
            out_specs=_row_spec(D_SB),
        ),
        out_shape=jax.ShapeDtypeStruct((db, 1, D_SB), BF16),
        compiler_params=_cparams(1),
        name="sb_sample_tail",
    )(page_table, more[:, 0, 0], q3, gates3, u, acc, carry,
      *([cache_kt] * len(tail_pages)), *([cache_vt] * len(tail_pages)))


def _moba_sample_body(pt_ref, qf_ref, kn_ref, vn_ref, bias_ref, bown_ref, g_ref, *refs, n_pages):
    del pt_ref
    k_refs, v_refs, o_ref = refs[:n_pages], refs[n_pages:2 * n_pages], refs[2 * n_pages]
    pages_per_block = MOBA_BLOCK // PAGE_SIZE
    n_past = n_pages // pages_per_block
    qf, keep = _head_rows(qf_ref[0], D_MB)
    q_hi, q_lo = _split_bf16(qf * Q_SCALE)

    scores = []
    for p in range(n_pages):
        kt = k_refs[p][0, 0].astype(BF16)
        scores.append(_dot(q_hi, kt) + _dot(q_lo, kt))
    lane = lax.broadcasted_iota(jnp.int32, (SUBLANES, LANES), 1)
    gate = jnp.zeros((SUBLANES, LANES), F32)
    for n in range(n_past):
        total = sum(jnp.sum(scores[p], axis=1, keepdims=True)
                    for p in range(n * pages_per_block, (n + 1) * pages_per_block))
        gate = jnp.where(lane == n, total * (1.0 / (Q_SCALE * MOBA_BLOCK)), gate)
    sel = _select_blocks(gate, n_past)

    kn = kn_ref[0].astype(BF16).astype(F32)
    s_own = jnp.sum(qf * Q_SCALE * kn, axis=1, keepdims=True) + bown_ref[:, 0:1]
    m = s_own
    for p in range(n_pages):
        s = scores[p] + bias_ref[:, p * PAGE_SIZE:(p + 1) * PAGE_SIZE]
        s = jnp.where(_block_selected(sel, p // pages_per_block), s, NEG_INF)
        scores[p] = s
        m = jnp.maximum(m, jnp.max(s, axis=1, keepdims=True))
    p_own = jnp.exp(s_own - m)
    l = p_own
    acc = p_own * vn_ref[0]
    for p in range(n_pages):
        w = jnp.exp(scores[p] - m)
        l = l + jnp.sum(w, axis=1, keepdims=True)
        acc = acc + _dot_nt(w.astype(BF16), v_refs[p][0, 0].astype(BF16))
    o_ref[0] = (_collapse_heads(acc / l, keep) * g_ref[0]).astype(BF16)


def _moba_sample(layer, page_table, qf, k_new, v_new, bias_keys, bias_own, gates3, cache_kt, cache_vt):
    db, n_pages = page_table.shape
    past = n_pages * PAGE_SIZE
    grid_spec = pltpu.PrefetchScalarGridSpec(
        num_scalar_prefetch=1,
        grid=(db,),
        in_specs=[_row_spec(D_MB), _row_spec(D_MB), _row_spec(D_MB),
                  pl.BlockSpec((SUBLANES, past), lambda b, pt: (0, 0)),
                  pl.BlockSpec((SUBLANES, LANES), lambda b, pt: (0, 0)),
                  _row_spec(D_MB, D_SB // D_MB)]
                 + _page_specs(layer, range(n_pages), D_MB) + _page_specs(layer, range(n_pages), D_MB),
        out_specs=_row_spec(D_MB),
    )
    r3 = lambda a: a.reshape(db, 1, D_MB)
    return pl.pallas_call(
        functools.partial(_moba_sample_body, n_pages=n_pages),
        grid_spec=grid_spec,
        out_shape=jax.ShapeDtypeStruct((db, 1, D_MB), BF16),
        compiler_params=_cparams(1),
        name="moba_sample",
    )(page_table, r3(qf), r3(k_new), r3(v_new), bias_keys, bias_own, gates3,
      *([cache_kt] * n_pages), *([cache_vt] * n_pages))


def _mem_sample_body(q_ref, mkt_ref, mvt_ref, g_ref, o_ref):
    qm, keep = _head_rows(q_ref[0], D_MEM)
    qm = qm.astype(BF16)
    s = _dot(qm, mkt_ref[0, 0].astype(BF16))
    p = jnp.exp(s - jnp.max(s, axis=1, keepdims=True))
    o = _dot_nt(p.astype(BF16), mvt_ref[0, 0].astype(BF16)) / jnp.sum(p, axis=1, keepdims=True)
    o_ref[0] = (_collapse_heads(o, keep) * g_ref[0]).astype(BF16)


def _mem_sample(layer, qb, gates3, mem_kt, mem_vt):
    db, n_mem = mem_kt.shape[1], mem_kt.shape[3]
    mem = pl.BlockSpec((1, 1, D_MEM, n_mem), lambda b: (layer, b, 0, 0))
    row = lambda off: pl.BlockSpec((1, 1, D_MEM), lambda b: (b, 0, off))
    return pl.pallas_call(
        _mem_sample_body,
        grid=(db,),
        in_specs=[row(0), mem, mem, row((D_SB + D_MB) // D_MEM)],
        out_specs=row(0),
        out_shape=jax.ShapeDtypeStruct((db, 1, D_MEM), BF16),
        compiler_params=_cparams(1),
        name="mem_sample",
    )(qb.reshape(db, 1, D_MEM), mem_kt, mem_vt, gates3)


def _upper_ones(n):
    i = np.arange(n)
    return jnp.asarray((i[:, None] > i[None, :]).astype(np.float32), dtype=BF16)


def _channel_major(cache, heads):
    lead = cache.shape[:-3]
    t = jnp.moveaxis(cache, -3, -1)
    return t.reshape(lead + (heads * HEAD_DIM, cache.shape[-3]))


def _token_major(x, heads):
    lead = x.shape[:-2]
    return jnp.moveaxis(x.reshape(lead + (heads, HEAD_DIM, x.shape[-1])), -1, -3)


def kernel(x_prompt, x_sample, cache_sb_k, cache_sb_v, cache_moba_k, cache_moba_v, cache_mem_k,
           cache_mem_v, page_table, mem_prompt, rel_bias, g_pre, g_post, g_mem, w_in, w_out, w_mem_kv):
    batch, seq, d_model = x_prompt.shape
    db, dec_seq, _ = x_sample.shape
    depth = w_in.shape[0]
    n_mem = mem_prompt.shape[1]
    n_pages = page_table.shape[1]
    past = n_pages * PAGE_SIZE
    assert dec_seq == 1 and seq % TOKEN_TILE == 0 and past % MOBA_BLOCK == 0
    assert past >= 2 * MOBA_BLOCK

    tm = TOKEN_TILE
    tiles = seq // tm
    u_block = _upper_ones(TOKEN_TILE)
    u_page = _upper_ones(PAGE_SIZE)
    p_outs, p_plan = _prompt_proj_plan()
    s_outs, s_plan = _sample_proj_plan()
    kv_outs, kv_plan = _mem_kv_plan()

    bias = _bias_tiles(rel_bias)
    pad = jnp.zeros((SUBLANES - H_MB, past), F32)
    bias_keys = jnp.concatenate(
        [jnp.concatenate([jnp.zeros((H_MB, past - MOBA_BLOCK), F32), bias[:, 1, 0, :]], axis=1), pad], axis=0)
    bias_own = jnp.concatenate(
        [jnp.broadcast_to(bias[:, 0, 0, 0:1], (H_MB, LANES)), pad[:, :LANES]], axis=0)

    sb_kt, sb_vt = _channel_major(cache_sb_k, H_SB), _channel_major(cache_sb_v, H_SB)
    mb_kt, mb_vt = _channel_major(cache_moba_k, H_MB), _channel_major(cache_moba_v, H_MB)
    mem_kt, mem_vt = _channel_major(cache_mem_k, H_MEM), _channel_major(cache_mem_v, H_MEM)

    xp = x_prompt.reshape(batch * seq, d_model)
    xs = x_sample.reshape(db, d_model)
    mem2 = mem_prompt.reshape(batch * n_mem, d_model)
    pt = page_table.astype(jnp.int32)
    outs = [[] for _ in range(10)]
    for l in range(depth):
        w_in_l = w_in[l].astype(BF16)
        w_kv_t = _kv_weight_t(w_in[l])
        w_out_l = w_out[l].astype(BF16)

        (kt_sb, vt_sb, kt_mb, vt_mb, ktb_sb, vtb_sb, ktb_mb, vtb_mb, qb_sb, qf_mb, qb_mem, gates) = _norm_proj(
            xp, g_pre[l], w_in_l, w_kv_t, p_outs, p_plan, tm, tiles)
        mkt, mvt, mktb, mvtb = _norm_proj(mem2, g_mem[l], None, w_mem_kv[l].T.astype(BF16),
                                          kv_outs, kv_plan, n_mem, 1)
        m_sb = _sb_prompt(qb_sb, ktb_sb, vtb_sb, gates, u_block, batch, seq)
        kmean = _kmean(kt_mb, batch, seq)
        m_mb = _moba_prompt(qf_mb, ktb_mb, vtb_mb, kmean, bias, gates, batch, seq)
        m_mem = _mem_prompt(qb_mem, mktb, mvtb, gates, batch, seq, 2 * tm)
        xp = _merge(m_sb, m_mb, m_mem, xp, w_out_l, g_post[l], 2 * tm)
        for i, (a, h) in enumerate(((kt_sb, H_SB), (vt_sb, H_SB), (kt_mb, H_MB), (vt_mb, H_MB),
                                    (mkt, H_MEM), (mvt, H_MEM))):
            outs[i].append(_token_major(a, h))

        (kt_sb, vt_sb, kt_mb, vt_mb, k_mb, v_mb, qb_sb, qf_mb, qb_mem, gates) = _norm_proj(
            xs, g_pre[l], w_in_l, w_kv_t, s_outs, s_plan, db, 1)
        gates3 = gates.reshape(db, 1, D_MIX)
        s_sb = _sb_sample(l, pt, qb_sb, gates3, u_page, sb_kt, sb_vt)
        s_mb = _moba_sample(l, pt, qf_mb, k_mb, v_mb, bias_keys, bias_own, gates3, mb_kt, mb_vt)
        s_mem = _mem_sample(l, qb_mem, gates3, mem_kt, mem_vt)
        xs = _merge(s_sb.reshape(db, D_SB), s_mb.reshape(db, D_MB), s_mem.reshape(db, D_MEM),
                    xs, w_out_l, g_post[l], db)
        for i, (a, h) in enumerate(((kt_sb, H_SB), (vt_sb, H_SB), (kt_mb, H_MB), (vt_mb, H_MB))):
            outs[6 + i].append(_token_major(a, h).reshape(db, 1, h, HEAD_DIM))

    return (xp.reshape(batch, seq, d_model), xs.reshape(db, 1, d_model)) + tuple(jnp.stack(o) for o in outs)
```

```python
import functools
import math

import numpy as np
import jax
import jax.numpy as jnp
from jax import lax
from jax.experimental import pallas as pl
from jax.experimental.pallas import tpu as pltpu

HEAD_DIM = 64
H_SB, H_MB, H_MEM = 6, 6, 4
D_SB, D_MB, D_MEM = H_SB * HEAD_DIM, H_MB * HEAD_DIM, H_MEM * HEAD_DIM
D_MIX = D_SB + D_MB + D_MEM
MOBA_BLOCK = 256
MOBA_TOPK = 3
PAGE_SIZE = 128
NUM_BUCKETS = 32
MAX_EXACT = 16
MAX_DISTANCE = 128
RMS_EPS = 1e-6
NEG_INF = -1e30
Q_SCALE = 1.0 / math.sqrt(HEAD_DIM)

LANES = 128
SUBLANES = 8
PAIR = LANES // HEAD_DIM
VMEM_LIMIT = 48 * 1024 * 1024
TOKEN_TILE = MOBA_BLOCK

SB_EXIT = 110.0
SB_SAMPLE_HEAD_PAGES = 2
SAMPLE_GROUP = 4
MEM_SAMPLE_GROUP = 8

BF16 = jnp.bfloat16
F32 = jnp.float32
_NT = (((1,), (1,)), ((), ()))


def _cparams(n_axes):
    return pltpu.CompilerParams(dimension_semantics=("arbitrary",) * n_axes,
                                vmem_limit_bytes=VMEM_LIMIT)


def _dot(a, b):
    return jnp.dot(a, b, preferred_element_type=F32)


def _dot_nt(a, b):
    return lax.dot_general(a, b, _NT, preferred_element_type=F32)


def _split_bf16(x):
    hi = x.astype(BF16)
    return hi, (x - hi.astype(F32)).astype(BF16)


def _split_dot(x, u):
    hi, lo = _split_bf16(x)
    return _dot(hi, u) + _dot(lo, u)


def _head_lane_mask(shape, hh):
    lane = lax.broadcasted_iota(jnp.int32, shape, len(shape) - 1)
    return (lane >= HEAD_DIM) if hh else (lane < HEAD_DIM)


def _log_keep(z):
    return jnp.minimum(-z, 0.0) - jnp.log(1.0 + jnp.exp(-jnp.abs(z)))


def _sb_blocks(qms, kts, vts, u, state, chain_of, masks=None):
    masks = masks or [None] * len(qms)
    zs = [_dot(qm, kt) for qm, kt in zip(qms, kts)]
    log_keeps = [_log_keep(z) if mask is None else jnp.where(mask, _log_keep(z), 0.0)
                 for z, mask in zip(zs, masks)]
    sums = [_split_dot(log_keep, u) for log_keep in log_keeps]
    state = list(state)
    for z, total, vt, c, mask in zip(zs, sums, vts, chain_of, masks):
        carry, acc = state[c]
        a = jnp.exp(z + total + carry)
        if mask is not None:
            a = jnp.where(mask, a, 0.0)
        state[c] = (carry + total[:, 0:1], acc + _dot_nt(a.astype(BF16), vt))
    return state


def _norm_proj_body(*refs, plan, has_w, has_wt):
    x_ref, g_ref = refs[:2]
    rest = list(refs[2:])
    w_ref = rest.pop(0) if has_w else None
    wt_ref = rest.pop(0) if has_wt else None
    out_refs = rest
    x = x_ref[...]
    ms = jnp.mean(x * x, axis=-1, keepdims=True)
    h = (x * lax.rsqrt(ms + RMS_EPS) * g_ref[...]).astype(BF16)
    for (transposed, lo, hi), writes in plan:
        p = _dot_nt(wt_ref[lo:hi, :], h) if transposed else _dot(h, w_ref[:, lo:hi])
        for oi, dlo, kind in writes:
            if kind == "f32":
                val = p
            elif kind == "bf16":
                val = p.astype(BF16)
            elif kind == "bf16_scaled":
                val = (p * Q_SCALE).astype(BF16)
            else:
                val = p * (1.0 / (1.0 + jnp.exp(-p)))
            if transposed:
                out_refs[oi][0] = val
            else:
                out_refs[oi][:, dlo:dlo + (hi - lo)] = val


def _norm_proj(x2, gain, w, wt, outs, plan, tm, tiles_per_batch):
    n, d = x2.shape
    nt = n // tm
    in_specs = [pl.BlockSpec((tm, d), lambda i: (i, 0)), pl.BlockSpec((1, d), lambda i: (0, 0))]
    args = [x2, gain.reshape(1, d)]
    for m in (w, wt):
        if m is not None:
            in_specs.append(pl.BlockSpec(m.shape, lambda i: (0, 0)))
            args.append(m)
    out_specs, out_shape = [], []
    for kind, width, dt in outs:
        if kind == "rows":
            out_specs.append(pl.BlockSpec((tm, width), lambda i: (i, 0)))
            out_shape.append(jax.ShapeDtypeStruct((n, width), dt))
        elif kind == "t_batch":
            out_specs.append(pl.BlockSpec((1, width, tm),
                                          lambda i: (i // tiles_per_batch, 0, i % tiles_per_batch)))
            out_shape.append(jax.ShapeDtypeStruct((nt // tiles_per_batch, width, tiles_per_batch * tm), dt))
        else:
            out_specs.append(pl.BlockSpec((1, width, tm), lambda i: (i, 0, 0)))
            out_shape.append(jax.ShapeDtypeStruct((nt, width, tm), dt))
    return pl.pallas_call(
        functools.partial(_norm_proj_body, plan=plan, has_w=w is not None, has_wt=wt is not None),
        grid=(nt,),
        in_specs=in_specs,
        out_specs=out_specs,
        out_shape=out_shape,
        compiler_params=_cparams(1),
        name="norm_proj",
    )(*args)


_SEG = {}
_c = 0
for _name, _w in (("q_sb", D_SB), ("k_sb", D_SB), ("v_sb", D_SB), ("z_sb", D_SB),
                  ("q_mb", D_MB), ("k_mb", D_MB), ("v_mb", D_MB), ("z_mb", D_MB),
                  ("q_mem", D_MEM), ("z_mem", D_MEM)):
    _SEG[_name] = (_c, _c + _w)
    _c += _w
_T_NAMES = ("k_sb", "v_sb", "k_mb", "v_mb", "q_mb")


def _kv_weight_t(w_in_l):
    return jnp.concatenate([w_in_l[:, _SEG[s][0]:_SEG[s][1]] for s in _T_NAMES], axis=1).T.astype(BF16)


def _prompt_proj_plan():
    t_rows = lambda i: (True, i * D_SB, (i + 1) * D_SB)
    outs = [("t_batch", D_SB, F32)] * 4
    outs += [("t_tile", D_SB, BF16)] * 3 + [("t_tile", D_MB, F32)]
    outs += [("rows", D_MB, BF16), ("rows", D_SB, BF16), ("rows", D_MEM, BF16),
             ("rows", D_MIX, F32)]
    plan = [(t_rows(0), ((0, 0, "f32"), (4, 0, "bf16"))),
            (t_rows(1), ((1, 0, "f32"), (5, 0, "bf16"))),
            (t_rows(2), ((2, 0, "f32"),)),
            (t_rows(3), ((3, 0, "f32"), (6, 0, "bf16"))),
            (t_rows(4), ((7, 0, "f32"),))]
    plan += [((False,) + _SEG["k_mb"], ((8, 0, "bf16"),)),
             ((False,) + _SEG["q_sb"], ((9, 0, "bf16_scaled"),)),
             ((False,) + _SEG["q_mem"], ((10, 0, "bf16_scaled"),)),
             ((False,) + _SEG["z_sb"], ((11, 0, "silu"),)),
             ((False,) + _SEG["z_mb"], ((11, D_SB, "silu"),)),
             ((False,) + _SEG["z_mem"], ((11, D_SB + D_MB, "silu"),))]
    return outs, tuple(plan)


def _sample_proj_plan():
    outs = [("t_batch", D_SB, F32)] * 4
    outs += [("rows", D_MB, F32), ("rows", D_MB, F32)]
    outs += [("rows", D_SB, BF16), ("rows", D_MB, F32), ("rows", D_MEM, BF16),
             ("rows", D_MIX, F32)]
    plan = [((True, i * D_SB, (i + 1) * D_SB), ((i, 0, "f32"),)) for i in range(4)]
    plan += [((False,) + _SEG["k_mb"], ((4, 0, "f32"),)),
             ((False,) + _SEG["v_mb"], ((5, 0, "f32"),)),
             ((False,) + _SEG["q_sb"], ((6, 0, "bf16_scaled"),)),
             ((False,) + _SEG["q_mb"], ((7, 0, "f32"),)),
             ((False,) + _SEG["q_mem"], ((8, 0, "bf16_scaled"),)),
             ((False,) + _SEG["z_sb"], ((9, 0, "silu"),)),
             ((False,) + _SEG["z_mb"], ((9, D_SB, "silu"),)),
             ((False,) + _SEG["z_mem"], ((9, D_SB + D_MB, "silu"),))]
    return outs, tuple(plan)


def _mem_kv_plan():
    outs = [("t_tile", D_MEM, F32), ("t_tile", D_MEM, F32), ("t_tile", D_MEM, BF16), ("t_tile", D_MEM, BF16)]
    plan = (((True, 0, D_MEM), ((0, 0, "f32"), (2, 0, "bf16"))),
            ((True, D_MEM, 2 * D_MEM), ((1, 0, "f32"), (3, 0, "bf16"))))
    return outs, plan


def _merge_body(msb_ref, mmb_ref, mmem_ref, x_ref, w_ref, g_ref, o_ref):
    y = (_dot(msb_ref[...], w_ref[0:D_SB, :])
         + _dot(mmb_ref[...], w_ref[D_SB:D_SB + D_MB, :])
         + _dot(mmem_ref[...], w_ref[D_SB + D_MB:D_MIX, :]))
    ms = jnp.mean(y * y, axis=-1, keepdims=True)
    o_ref[...] = x_ref[...] + y * lax.rsqrt(ms + RMS_EPS) * g_ref[...]


def _merge(msb, mmb, mmem, x2, w_out_bf16, g_post, tm):
    n, d = x2.shape
    row = lambda w: pl.BlockSpec((tm, w), lambda i: (i, 0))
    return pl.pallas_call(
        _merge_body,
        grid=(n // tm,),
        in_specs=[row(D_SB), row(D_MB), row(D_MEM), row(d),
                  pl.BlockSpec((D_MIX, d), lambda i: (0, 0)),
                  pl.BlockSpec((1, d), lambda i: (0, 0))],
        out_specs=row(d),
        out_shape=jax.ShapeDtypeStruct((n, d), F32),
        compiler_params=_cparams(1),
        name="merge",
    )(msb, mmb, mmem, x2, w_out_bf16, g_post.reshape(1, d))


def _sb_prompt_body(q_ref, kt_ref, vt_ref, u_ref, g_ref, o_ref):
    tq = TOKEN_TILE
    qi = pl.program_id(2)
    q = q_ref[...]
    row = lax.broadcasted_iota(jnp.int32, (tq, tq), 0)
    col = lax.broadcasted_iota(jnp.int32, (tq, tq), 1)
    heads = [_head_lane_mask((tq, LANES), hh) for hh in range(PAIR)]
    qms = [jnp.where(head, q, jnp.zeros_like(q)) for head in heads]

    def blocks(js, state, masks=None):
        return _sb_blocks(qms * len(js), [kt_ref[j] for j in js for _ in range(PAIR)],
                          [vt_ref[j] for j in js for _ in range(PAIR)], u_ref[...], state,
                          list(range(PAIR)) * len(js), masks)

    def carry_max(state):
        return functools.reduce(jnp.maximum, [jnp.max(carry) for carry, _ in state])

    diag = [col < row] * PAIR
    zero = [(jnp.zeros((tq, 1), F32), jnp.zeros((tq, LANES), F32))] * PAIR
    state = lax.cond(qi > 0, lambda: blocks([qi, qi - 1], zero, diag + [None] * PAIR),
                     lambda: blocks([qi], zero, diag))

    def cond(loop):
        j, cmax, _ = loop
        return jnp.logical_and(j >= 0, cmax > -SB_EXIT)

    def body(loop):
        j, _, state = loop
        state = blocks([j], state)
        return j - 1, carry_max(state), state

    state = lax.while_loop(cond, body, (qi - 2, carry_max(state), state))[2]
    out = jnp.where(heads[0], state[0][1], state[1][1])
    o_ref[...] = (out * g_ref[...]).astype(BF16)


def _prompt_specs(nq, g_off):
    tq = TOKEN_TILE
    tile = lambda off: pl.BlockSpec((tq, LANES), lambda b, p, i: (b * nq + i, p + off))
    keys = pl.BlockSpec((nq, LANES, tq), lambda b, p, i: (b, p, 0))
    return tile(0), keys, tile(g_off)


def _sb_prompt(qb, ktb, vtb, gates, u, batch, seq):
    tq = TOKEN_TILE
    nq = seq // tq
    tile, keys, gate_tile = _prompt_specs(nq, 0)
    return pl.pallas_call(
        _sb_prompt_body,
        grid=(batch, D_SB // LANES, nq),
        in_specs=[tile, keys, keys, pl.BlockSpec((tq, tq), lambda b, p, i: (0, 0)), gate_tile],
        out_specs=tile,
        out_shape=jax.ShapeDtypeStruct((batch * seq, D_SB), BF16),
        compiler_params=_cparams(3),
        name="sb_prompt",
    )(qb, ktb, vtb, u, gates)


def _kmean_body(kt_ref, o_ref):
    nb = pl.program_id(1)
    mean = jnp.sum(kt_ref[0], axis=1, keepdims=True) * (1.0 / MOBA_BLOCK)
    lane = lax.broadcasted_iota(jnp.int32, o_ref.shape[1:], 1)

    @pl.when(nb == 0)
    def _():
        o_ref[0] = jnp.zeros(o_ref.shape[1:], F32)

    o_ref[0] = jnp.where(lane == nb, mean, o_ref[0])


def _kmean(kt_f32, batch, seq):
    nb = seq // MOBA_BLOCK
    assert nb <= LANES
    return pl.pallas_call(
        _kmean_body,
        grid=(batch, nb),
        in_specs=[pl.BlockSpec((1, D_MB, MOBA_BLOCK), lambda b, n: (b, 0, n))],
        out_specs=pl.BlockSpec((1, D_MB, LANES), lambda b, n: (b, 0, 0)),
        out_shape=jax.ShapeDtypeStruct((batch, D_MB, LANES), F32),
        compiler_params=_cparams(2),
        name="kmean",
    )(kt_f32)


def _t5_bucket_np(dist):
    n = np.maximum(dist, 0)
    nf = np.maximum(n, 1).astype(np.float32)
    large = MAX_EXACT + (np.log(nf / np.float32(MAX_EXACT)) / np.float32(math.log(MAX_DISTANCE / MAX_EXACT))
                         * np.float32(NUM_BUCKETS - MAX_EXACT)).astype(np.int32)
    large = np.minimum(large, NUM_BUCKETS - 1)
    return np.where(n < MAX_EXACT, n, large).astype(np.int32)


def _bias_tiles_body(tab_ref, bucket_ref, o_ref):
    h = pl.program_id(0)
    for t in range(2):
        bucket = bucket_ref[t]
        acc = jnp.zeros(bucket.shape, F32)
        for b in range(NUM_BUCKETS):
            acc = jnp.where(bucket == b, tab_ref[b, h], acc)
        o_ref[0, t] = acc - tab_ref[NUM_BUCKETS - 1, h]


def _bias_tiles(rel_bias):
    assert MOBA_BLOCK >= MAX_DISTANCE
    i = np.arange(MOBA_BLOCK)[None, :]
    j = np.arange(MOBA_BLOCK)[:, None]
    buckets = np.stack([_t5_bucket_np(i - j), _t5_bucket_np(MOBA_BLOCK + i - j)])
    return pl.pallas_call(
        _bias_tiles_body,
        grid=(H_MB,),
        in_specs=[pl.BlockSpec(memory_space=pltpu.SMEM),
                  pl.BlockSpec((2, MOBA_BLOCK, MOBA_BLOCK), lambda h: (0, 0, 0))],
        out_specs=pl.BlockSpec((1, 2, MOBA_BLOCK, MOBA_BLOCK), lambda h: (h, 0, 0, 0)),
        out_shape=jax.ShapeDtypeStruct((H_MB, 2, MOBA_BLOCK, MOBA_BLOCK), F32),
        compiler_params=_cparams(1),
        name="bias_tiles",
    )(rel_bias, jnp.asarray(buckets))


def _select_columns(gates):
    picked = []
    for n, g in enumerate(gates):
        beaten_by = [jnp.where((other > g) if m > n else (other >= g), 1.0, 0.0)
                     for m, other in enumerate(gates) if m != n]
        picked.append(functools.reduce(jnp.add, beaten_by) < MOBA_TOPK)
    return picked


def _select_blocks_t(gate, n_past):
    row = lax.broadcasted_iota(jnp.int32, gate.shape, 0)
    row_f = row.astype(F32)
    g = jnp.where(row < n_past, gate, NEG_INF)
    sel = jnp.zeros(gate.shape, F32)
    for _ in range(MOBA_TOPK):
        m = jnp.max(g, axis=0, keepdims=True)
        first = jnp.min(jnp.where(g == m, row_f, float(gate.shape[0])), axis=0, keepdims=True)
        pick = row_f == first
        sel = jnp.where(pick, 1.0, sel)
        g = jnp.where(pick, -jnp.inf, g)
    return jnp.where(row < n_past, sel, 0.0)


def _moba_prompt_body(qt_ref, k_ref, vt_ref, km_ref, bias_ref, g_ref, o_ref, pen_ref):
    c0 = pl.program_id(2)
    tq = MOBA_BLOCK
    n_rows = pen_ref.shape[1]
    qt = qt_ref[0]
    qtb = (qt * Q_SCALE).astype(BF16)
    km_t = km_ref[0].T[:n_rows]
    chan = lax.broadcasted_iota(jnp.int32, (LANES, tq), 0)
    key = lax.broadcasted_iota(jnp.int32, (tq, tq), 0)
    qry = lax.broadcasted_iota(jnp.int32, (tq, tq), 1)
    ones = jnp.ones((2 * SUBLANES, tq), BF16)
    prev = jnp.maximum(c0 - 1, 0)

    def keys_of(n):
        return k_ref[pl.ds(pl.multiple_of(n * tq, tq), tq), :]

    def values_of(hh, n):
        return jnp.concatenate([vt_ref[n][hh * HEAD_DIM:(hh + 1) * HEAD_DIM, :], ones], axis=0)

    def update(st, s, pen, vals):
        m, l, acc = st
        m_new = jnp.maximum(m, jnp.max(s, axis=0, keepdims=True) + pen)
        alpha = jnp.exp(m - m_new)
        pv = _dot(vals, jnp.exp(s - (m_new - pen)).astype(BF16))
        return m_new, alpha * l + pv[HEAD_DIM:HEAD_DIM + 1, :], alpha * acc + pv[:HEAD_DIM, :]

    heads = [(chan >= HEAD_DIM) if hh else (chan < HEAD_DIM) for hh in range(PAIR)]
    qts = [jnp.where(head, qtb, jnp.zeros_like(qtb)) for head in heads]
    gates = [jnp.dot(km_t, jnp.where(head, qt, 0.0), precision=lax.Precision.HIGHEST,
                     preferred_element_type=F32) for head in heads]
    s_own = [_dot(keys_of(c0), qts[hh]) for hh in range(PAIR)]
    s_prev = [_dot(keys_of(prev), qts[hh]) for hh in range(PAIR)]
    state = []
    for hh in range(PAIR):
        pen_ref[hh] = jnp.where(_select_blocks_t(gates[hh], c0) > 0.5, 0.0, NEG_INF)
        pen = pen_ref[hh, pl.ds(prev, 1), :]
        s_o = jnp.where(key <= qry, s_own[hh] + bias_ref[hh, 0], NEG_INF)
        s_p = s_prev[hh] + bias_ref[hh, 1]
        m = jnp.maximum(jnp.max(s_o, axis=0, keepdims=True), jnp.max(s_p, axis=0, keepdims=True) + pen)
        pv = (_dot(values_of(hh, c0), jnp.exp(s_o - m).astype(BF16))
              + _dot(values_of(hh, prev), jnp.exp(s_p - (m - pen)).astype(BF16)))
        state += [m, pv[HEAD_DIM:HEAD_DIM + 1, :], pv[:HEAD_DIM, :]]

    def body(i, st):
        blocks = []
        for n in (2 * i, 2 * i + 1):
            real = n < prev
            blocks.append((jnp.minimum(n, prev - 1), jnp.where(real, n, n_rows - 1)))
        scores = [[_dot(keys_of(n), qts[hh]) for hh in range(PAIR)] for n, _ in blocks]
        st = list(st)
        for (n, pen_row), sc in zip(blocks, scores):
            for hh in range(PAIR):
                st[3 * hh:3 * hh + 3] = update(st[3 * hh:3 * hh + 3], sc[hh],
                                               pen_ref[hh, pl.ds(pen_row, 1), :], values_of(hh, n))
        return tuple(st)

    final = lax.fori_loop(0, (prev + 1) // 2, body, tuple(state))
    out_t = jnp.concatenate([final[2] / final[1], final[5] / final[4]], axis=0)
    o_ref[...] = (out_t.T * g_ref[...]).astype(BF16)


def _moba_prompt(qt, kb, vtb, kmean, bias, gates, batch, seq):
    tq = MOBA_BLOCK
    nq = seq // tq
    n_rows = -(-nq // SUBLANES) * SUBLANES
    tile, keys_t, gate_tile = _prompt_specs(nq, D_SB // LANES)
    return pl.pallas_call(
        _moba_prompt_body,
        grid=(batch, D_MB // LANES, nq),
        in_specs=[pl.BlockSpec((1, LANES, tq), lambda b, p, i: (b * nq + i, p, 0)),
                  pl.BlockSpec((seq, LANES), lambda b, p, i: (b, p)),
                  keys_t,
                  pl.BlockSpec((1, LANES, LANES), lambda b, p, i: (b, p, 0)),
                  pl.BlockSpec((PAIR, 2, tq, tq), lambda b, p, i: (p, 0, 0, 0)),
                  gate_tile],
        out_specs=tile,
        out_shape=jax.ShapeDtypeStruct((batch * seq, D_MB), BF16),
        scratch_shapes=[pltpu.VMEM((PAIR, n_rows, tq), F32)],
        compiler_params=_cparams(3),
        name="moba_prompt",
    )(qt, kb, vtb, kmean, bias, gates)


def _mem_prompt_body(q_ref, mkt_ref, mvt_ref, g_ref, o_ref):
    tq = q_ref.shape[0]
    for pr in range(D_MEM // LANES):
        chans = slice(pr * LANES, (pr + 1) * LANES)
        q = q_ref[:, chans]
        mkt = mkt_ref[0, chans, :]
        mvt = mvt_ref[0, chans, :]
        out = jnp.zeros((tq, LANES), F32)
        for hh in range(PAIR):
            head = _head_lane_mask((tq, LANES), hh)
            s = _dot(jnp.where(head, q, jnp.zeros_like(q)), mkt)
            p = jnp.exp(s - jnp.max(s, axis=1, keepdims=True))
            o = _dot_nt(p.astype(BF16), mvt) / jnp.sum(p, axis=1, keepdims=True)
            out = jnp.where(head, o, out)
        o_ref[:, chans] = (out * g_ref[:, chans]).astype(BF16)


def _mem_prompt(qb, mktb, mvtb, gates, batch, seq, tq):
    nq = seq // tq
    n_mem = mktb.shape[2]
    g_off = (D_SB + D_MB) // D_MEM
    mem = pl.BlockSpec((1, D_MEM, n_mem), lambda b, i: (b, 0, 0))
    return pl.pallas_call(
        _mem_prompt_body,
        grid=(batch, nq),
        in_specs=[pl.BlockSpec((tq, D_MEM), lambda b, i: (b * nq + i, 0)), mem, mem,
                  pl.BlockSpec((tq, D_MEM), lambda b, i: (b * nq + i, g_off))],
        out_specs=pl.BlockSpec((tq, D_MEM), lambda b, i: (b * nq + i, 0)),
        out_shape=jax.ShapeDtypeStruct((batch * seq, D_MEM), BF16),
        compiler_params=_cparams(2),
        name="mem_prompt",
    )(qb, mktb, mvtb, gates)


def _head_rows(x_row, width):
    r = lax.broadcasted_iota(jnp.int32, (SUBLANES, width), 0)
    lane = lax.broadcasted_iota(jnp.int32, (SUBLANES, width), 1)
    keep = (lane >= r * HEAD_DIM) & (lane < (r + 1) * HEAD_DIM)
    return jnp.where(keep, jnp.broadcast_to(x_row.astype(F32), (SUBLANES, width)), 0.0), keep


def _collapse_heads(x, keep):
    return jnp.sum(jnp.where(keep, x, 0.0), axis=0, keepdims=True)


def _sb_sample_chain(qm, u, k_refs, v_refs, carry, acc):
    n = len(k_refs)
    return _sb_blocks([qm] * n, [k_ref[0, 0].astype(BF16) for k_ref in reversed(k_refs)],
                      [v_ref[0, 0].astype(BF16) for v_ref in reversed(v_refs)], u, [(carry, acc)], [0] * n)[0]


def _sb_sample_head_body(pt_ref, q_ref, g_ref, u_ref, *refs, n_pages, group):
    del pt_ref
    n = n_pages * group
    k_refs, v_refs = refs[:n], refs[n:2 * n]
    o_ref, acc_ref, carry_ref, more_ref = refs[2 * n:]
    rows = [_head_rows(q_ref[g], D_SB) for g in range(group)]
    order = [(g, g * n_pages + p) for p in reversed(range(n_pages)) for g in range(group)]
    state = _sb_blocks([rows[g][0].astype(BF16) for g, _ in order],
                       [k_refs[i][0, 0].astype(BF16) for _, i in order],
                       [v_refs[i][0, 0].astype(BF16) for _, i in order], u_ref[...],
                       [(jnp.zeros((SUBLANES, 1), F32), jnp.zeros((SUBLANES, D_SB), F32))] * group,
                       [g for g, _ in order])
    live = lax.broadcasted_iota(jnp.int32, (SUBLANES, 1), 0) < H_SB
    for g, (carry, acc) in enumerate(state):
        o_ref[g] = (_collapse_heads(acc, rows[g][1]) * g_ref[g]).astype(BF16)
        acc_ref[g] = acc
        carry_ref[g] = jnp.broadcast_to(carry, (SUBLANES, LANES))
        more = jnp.max(jnp.where(live, carry, -jnp.inf)) > -SB_EXIT
        more_ref[g] = jnp.broadcast_to(more.astype(jnp.int32), (SUBLANES, LANES))


def _sb_sample_tail_body(pt_ref, more_ref, q_ref, g_ref, u_ref, acc_ref, carry_ref, *refs, n_pages):
    del pt_ref
    b = pl.program_id(0)
    k_refs, v_refs, o_ref = refs[:n_pages], refs[n_pages:2 * n_pages], refs[2 * n_pages]
    qm, keep = _head_rows(q_ref[0], D_SB)
    acc0 = acc_ref[0]

    def rest():
        return _sb_sample_chain(qm.astype(BF16), u_ref[...], k_refs, v_refs, carry_ref[0][:, 0:1], acc0)[1]

    acc = lax.cond(more_ref[b] != 0, rest, lambda: acc0)
    o_ref[0] = (_collapse_heads(acc, keep) * g_ref[0]).astype(BF16)


def _page_specs(layer, pages, width, gated=False):
    def spec(p):
        if gated:
            return pl.BlockSpec((1, 1, width, PAGE_SIZE),
                                lambda b, pt, more: (layer, jnp.where(more[b] != 0, pt[b, p], 0), 0, 0))
        return pl.BlockSpec((1, 1, width, PAGE_SIZE), lambda b, pt: (layer, pt[b, p], 0, 0))
    return [spec(p) for p in pages]


def _row_spec(width, off=0):
    return pl.BlockSpec((1, 1, width), lambda b, *_: (b, 0, off))


def _sb_sample(layer, page_table, qb, gates3, u, cache_kt, cache_vt):
    db, n_pages = page_table.shape
    group = SAMPLE_GROUP
    q3 = qb.reshape(db, 1, D_SB)
    head_pages = range(n_pages - SB_SAMPLE_HEAD_PAGES, n_pages)
    tail_pages = range(0, n_pages - SB_SAMPLE_HEAD_PAGES)
    u_spec = pl.BlockSpec((PAGE_SIZE, PAGE_SIZE), lambda b, *_: (0, 0))
    rows = lambda w, n=1: pl.BlockSpec((group, n, w), lambda b, *_: (b, 0, 0))
    group_pages = [pl.BlockSpec((1, 1, D_SB, PAGE_SIZE),
                                functools.partial(lambda b, pt, g, p: (layer, pt[b * group + g, p], 0, 0), g=g, p=p))
                   for g in range(group) for p in head_pages]
    n_head = len(group_pages)
    out, acc, carry, more = pl.pallas_call(
        functools.partial(_sb_sample_head_body, n_pages=len(head_pages), group=group),
        grid_spec=pltpu.PrefetchScalarGridSpec(
            num_scalar_prefetch=1,
            grid=(db // group,),
            in_specs=[rows(D_SB), rows(D_SB), u_spec] + group_pages + group_pages,
            out_specs=[rows(D_SB), rows(D_SB, SUBLANES), rows(LANES, SUBLANES), rows(LANES, SUBLANES)],
        ),
        out_shape=[jax.ShapeDtypeStruct((db, 1, D_SB), BF16),
                   jax.ShapeDtypeStruct((db, SUBLANES, D_SB), F32),
                   jax.ShapeDtypeStruct((db, SUBLANES, LANES), F32),
                   jax.ShapeDtypeStruct((db, SUBLANES, LANES), jnp.int32)],
        compiler_params=_cparams(1),
        name="sb_sample_head",
    )(page_table, q3, gates3, u, *([cache_kt] * n_head), *([cache_vt] * n_head))
    more = more[:, 0, 0]

    def tail():
        state = lambda w: pl.BlockSpec((1, SUBLANES, w), lambda b, *_: (b, 0, 0))
        return pl.pallas_call(
            functools.partial(_sb_sample_tail_body, n_pages=len(tail_pages)),
            grid_spec=pltpu.PrefetchScalarGridSpec(
                num_scalar_prefetch=2,
                grid=(db,),
                in_specs=[_row_spec(D_SB), _row_spec(D_SB), u_spec, state(D_SB), state(LANES)]
                         + _page_specs(layer, tail_pages, D_SB, gated=True)
                         + _page_specs(layer, tail_pages, D_SB, gated=True),
                out_specs=_row_spec(D_SB),
            ),
            out_shape=jax.ShapeDtypeStruct((db, 1, D_SB), BF16),
            compiler_params=_cparams(1),
            name="sb_sample_tail",
        )(page_table, more, q3, gates3, u, acc, carry,
          *([cache_kt] * len(tail_pages)), *([cache_vt] * len(tail_pages)))

    return lax.cond(jnp.any(more != 0), tail, lambda: out)


def _moba_sample_body(pt_ref, qf_ref, kn_ref, vn_ref, bias_ref, bown_ref, g_ref, *refs, n_pages):
    del pt_ref
    k_refs, v_refs, o_ref = refs[:n_pages], refs[n_pages:2 * n_pages], refs[2 * n_pages]
    pages_per_block = MOBA_BLOCK // PAGE_SIZE
    n_past = n_pages // pages_per_block
    qf, keep = _head_rows(qf_ref[0], D_MB)
    q_hi, q_lo = _split_bf16(qf * Q_SCALE)

    scores = []
    for p in range(n_pages):
        kt = k_refs[p][0, 0].astype(BF16)
        scores.append(_dot(q_hi, kt) + _dot(q_lo, kt))
    gates = [jnp.sum(functools.reduce(jnp.add, scores[n * pages_per_block:(n + 1) * pages_per_block]),
                     axis=1, keepdims=True) * (1.0 / (Q_SCALE * MOBA_BLOCK)) for n in range(n_past)]
    sel = _select_columns(gates)

    kn = kn_ref[0].astype(BF16).astype(F32)
    s_own = jnp.sum(qf * Q_SCALE * kn, axis=1, keepdims=True) + bown_ref[:, 0:1]
    masked = [jnp.where(sel[p // pages_per_block], scores[p] + bias_ref[:, p * PAGE_SIZE:(p + 1) * PAGE_SIZE],
                        NEG_INF) for p in range(n_pages)]
    m = jnp.maximum(s_own, jnp.max(functools.reduce(jnp.maximum, masked), axis=1, keepdims=True))
    p_own = jnp.exp(s_own - m)
    weights = [jnp.exp(s - m) for s in masked]
    l = p_own + jnp.sum(functools.reduce(jnp.add, weights), axis=1, keepdims=True)
    acc = p_own * vn_ref[0]
    for w, v_ref in zip(weights, v_refs):
        acc = acc + _dot_nt(w.astype(BF16), v_ref[0, 0].astype(BF16))
    o_ref[0] = (_collapse_heads(acc / l, keep) * g_ref[0]).astype(BF16)


def _moba_sample(layer, page_table, qf, k_new, v_new, bias_keys, bias_own, gates3, cache_kt, cache_vt):
    db, n_pages = page_table.shape
    past = n_pages * PAGE_SIZE
    grid_spec = pltpu.PrefetchScalarGridSpec(
        num_scalar_prefetch=1,
        grid=(db,),
        in_specs=[_row_spec(D_MB), _row_spec(D_MB), _row_spec(D_MB),
                  pl.BlockSpec((SUBLANES, past), lambda b, pt: (0, 0)),
                  pl.BlockSpec((SUBLANES, LANES), lambda b, pt: (0, 0)),
                  _row_spec(D_MB, D_SB // D_MB)]
                 + _page_specs(layer, range(n_pages), D_MB) + _page_specs(layer, range(n_pages), D_MB),
        out_specs=_row_spec(D_MB),
    )
    r3 = lambda a: a.reshape(db, 1, D_MB)
    return pl.pallas_call(
        functools.partial(_moba_sample_body, n_pages=n_pages),
        grid_spec=grid_spec,
        out_shape=jax.ShapeDtypeStruct((db, 1, D_MB), BF16),
        compiler_params=_cparams(1),
        name="moba_sample",
    )(page_table, r3(qf), r3(k_new), r3(v_new), bias_keys, bias_own, gates3,
      *([cache_kt] * n_pages), *([cache_vt] * n_pages))


def _mem_sample_body(q_ref, mkt_ref, mvt_ref, g_ref, o_ref, *, group):
    rows = [_head_rows(q_ref[g], D_MEM) for g in range(group)]
    scores = [_dot(rows[g][0].astype(BF16), mkt_ref[0, g].astype(BF16)) for g in range(group)]
    probs = [jnp.exp(s - jnp.max(s, axis=1, keepdims=True)) for s in scores]
    outs = [_dot_nt(p.astype(BF16), mvt_ref[0, g].astype(BF16)) for g, p in enumerate(probs)]
    for g in range(group):
        o = outs[g] / jnp.sum(probs[g], axis=1, keepdims=True)
        o_ref[g] = (_collapse_heads(o, rows[g][1]) * g_ref[g]).astype(BF16)


def _mem_sample(layer, qb, gates3, mem_kt, mem_vt):
    db, n_mem = mem_kt.shape[1], mem_kt.shape[3]
    group = MEM_SAMPLE_GROUP
    mem = pl.BlockSpec((1, group, D_MEM, n_mem), lambda b: (layer, b, 0, 0))
    row = lambda off: pl.BlockSpec((group, 1, D_MEM), lambda b: (b, 0, off))
    return pl.pallas_call(
        functools.partial(_mem_sample_body, group=group),
        grid=(db // group,),
        in_specs=[row(0), mem, mem, row((D_SB + D_MB) // D_MEM)],
        out_specs=row(0),
        out_shape=jax.ShapeDtypeStruct((db, 1, D_MEM), BF16),
        compiler_params=_cparams(1),
        name="mem_sample",
    )(qb.reshape(db, 1, D_MEM), mem_kt, mem_vt, gates3)


def _upper_ones(n):
    i = np.arange(n)
    return jnp.asarray((i[:, None] >= i[None, :]).astype(np.float32), dtype=BF16)


def _channel_major(cache, heads):
    lead = cache.shape[:-3]
    t = jnp.moveaxis(cache, -3, -1)
    return t.reshape(lead + (heads * HEAD_DIM, cache.shape[-3]))


def _token_major(x, heads):
    lead = x.shape[:-2]
    return jnp.moveaxis(x.reshape(lead + (heads, HEAD_DIM, x.shape[-1])), -1, -3)


def kernel(x_prompt, x_sample, cache_sb_k, cache_sb_v, cache_moba_k, cache_moba_v, cache_mem_k,
           cache_mem_v, page_table, mem_prompt, rel_bias, g_pre, g_post, g_mem, w_in, w_out, w_mem_kv):
    batch, seq, d_model = x_prompt.shape
    db, dec_seq, _ = x_sample.shape
    depth = w_in.shape[0]
    n_mem = mem_prompt.shape[1]
    n_pages = page_table.shape[1]
    past = n_pages * PAGE_SIZE
    assert dec_seq == 1 and seq % TOKEN_TILE == 0 and past % MOBA_BLOCK == 0
    assert past >= 2 * MOBA_BLOCK and db % SAMPLE_GROUP == 0 and db % MEM_SAMPLE_GROUP == 0

    tm = TOKEN_TILE
    tiles = seq // tm
    u_block = _upper_ones(TOKEN_TILE)
    u_page = _upper_ones(PAGE_SIZE)
    p_outs, p_plan = _prompt_proj_plan()
    s_outs, s_plan = _sample_proj_plan()
    kv_outs, kv_plan = _mem_kv_plan()

    bias = _bias_tiles(rel_bias)
    pad = jnp.zeros((SUBLANES - H_MB, past), F32)
    bias_keys = jnp.concatenate(
        [jnp.concatenate([jnp.zeros((H_MB, past - MOBA_BLOCK), F32), bias[:, 1, :, 0]], axis=1), pad], axis=0)
    bias_own = jnp.concatenate(
        [jnp.broadcast_to(bias[:, 0, 0, 0:1], (H_MB, LANES)), pad[:, :LANES]], axis=0)

    sb_kt, sb_vt = _channel_major(cache_sb_k, H_SB), _channel_major(cache_sb_v, H_SB)
    mb_kt, mb_vt = _channel_major(cache_moba_k, H_MB), _channel_major(cache_moba_v, H_MB)
    mem_kt, mem_vt = _channel_major(cache_mem_k, H_MEM), _channel_major(cache_mem_v, H_MEM)

    xp = x_prompt.reshape(batch * seq, d_model)
    xs = x_sample.reshape(db, d_model)
    mem2 = mem_prompt.reshape(batch * n_mem, d_model)
    pt = page_table.astype(jnp.int32)
    outs = [[] for _ in range(10)]
    for l in range(depth):
        w_in_l = w_in[l].astype(BF16)
        w_kv_t = _kv_weight_t(w_in[l])
        w_out_l = w_out[l].astype(BF16)

        (kt_sb, vt_sb, kt_mb, vt_mb, ktb_sb, vtb_sb, vtb_mb, qt_mb, kb_mb, qb_sb, qb_mem, gates) = _norm_proj(
            xp, g_pre[l], w_in_l, w_kv_t, p_outs, p_plan, tm, tiles)
        mkt, mvt, mktb, mvtb = _norm_proj(mem2, g_mem[l], None, w_mem_kv[l].T.astype(BF16),
                                          kv_outs, kv_plan, n_mem, 1)
        m_sb = _sb_prompt(qb_sb, ktb_sb, vtb_sb, gates, u_block, batch, seq)
        kmean = _kmean(kt_mb, batch, seq)
        m_mb = _moba_prompt(qt_mb, kb_mb, vtb_mb, kmean, bias, gates, batch, seq)
        m_mem = _mem_prompt(qb_mem, mktb, mvtb, gates, batch, seq, 2 * tm)
        xp = _merge(m_sb, m_mb, m_mem, xp, w_out_l, g_post[l], 2 * tm)
        for i, (a, h) in enumerate(((kt_sb, H_SB), (vt_sb, H_SB), (kt_mb, H_MB), (vt_mb, H_MB),
                                    (mkt, H_MEM), (mvt, H_MEM))):
            outs[i].append(_token_major(a, h))

        (kt_sb, vt_sb, kt_mb, vt_mb, k_mb, v_mb, qb_sb, qf_mb, qb_mem, gates) = _norm_proj(
            xs, g_pre[l], w_in_l, w_kv_t, s_outs, s_plan, db, 1)
        gates3 = gates.reshape(db, 1, D_MIX)
        s_sb = _sb_sample(l, pt, qb_sb, gates3, u_page, sb_kt, sb_vt)
        s_mb = _moba_sample(l, pt, qf_mb, k_mb, v_mb, bias_keys, bias_own, gates3, mb_kt, mb_vt)
        s_mem = _mem_sample(l, qb_mem, gates3, mem_kt, mem_vt)
        xs = _merge(s_sb.reshape(db, D_SB), s_mb.reshape(db, D_MB), s_mem.reshape(db, D_MEM),
                    xs, w_out_l, g_post[l], db)
        for i, (a, h) in enumerate(((kt_sb, H_SB), (vt_sb, H_SB), (kt_mb, H_MB), (vt_mb, H_MB))):
            outs[6 + i].append(_token_major(a, h).reshape(db, 1, h, HEAD_DIM))

    return (xp.reshape(batch, seq, d_model), xs.reshape(db, 1, d_model)) + tuple(jnp.stack(o) for o in outs)
```

```python
import functools
import math

import numpy as np
import jax
import jax.numpy as jnp
from jax import lax
from jax.experimental import pallas as pl
from jax.experimental.pallas import tpu as pltpu

HEAD_DIM = 64
H_SB, H_MB, H_MEM = 6, 6, 4
D_SB, D_MB, D_MEM = H_SB * HEAD_DIM, H_MB * HEAD_DIM, H_MEM * HEAD_DIM
D_MIX = D_SB + D_MB + D_MEM
MOBA_BLOCK = 256
MOBA_TOPK = 3
PAGE_SIZE = 128
NUM_BUCKETS = 32
MAX_EXACT = 16
MAX_DISTANCE = 128
RMS_EPS = 1e-6
NEG_INF = -1e30
Q_SCALE = 1.0 / math.sqrt(HEAD_DIM)

LANES = 128
SUBLANES = 8
PAIR = LANES // HEAD_DIM
VMEM_LIMIT = 48 * 1024 * 1024
TOKEN_TILE = MOBA_BLOCK

SB_EXIT = 110.0
SB_SAMPLE_HEAD_PAGES = 2
MOBA_BLOCKS_PER_TRIP = 3
SAMPLE_GROUP = 4
MEM_SAMPLE_GROUP = 8

BF16 = jnp.bfloat16
F32 = jnp.float32
_NT = (((1,), (1,)), ((), ()))


def _cparams(n_axes):
    return pltpu.CompilerParams(dimension_semantics=("arbitrary",) * n_axes,
                                vmem_limit_bytes=VMEM_LIMIT)


def _dot(a, b):
    return jnp.dot(a, b, preferred_element_type=F32)


def _dot_nt(a, b):
    return lax.dot_general(a, b, _NT, preferred_element_type=F32)


def _split_bf16(x):
    hi = x.astype(BF16)
    return hi, (x - hi.astype(F32)).astype(BF16)


def _split_dot(x, u):
    hi, lo = _split_bf16(x)
    return _dot(hi, u) + _dot(lo, u)


def _head_lane_mask(shape, hh):
    lane = lax.broadcasted_iota(jnp.int32, shape, len(shape) - 1)
    return (lane >= HEAD_DIM) if hh else (lane < HEAD_DIM)


def _log_keep(z):
    return jnp.minimum(-z, 0.0) - jnp.log(1.0 + jnp.exp(-jnp.abs(z)))


def _sb_blocks(qms, kts, vts, u, state, chain_of, masks=None):
    masks = masks or [None] * len(qms)
    zs = [_dot(qm, kt) for qm, kt in zip(qms, kts)]
    log_keeps = [_log_keep(z) if mask is None else jnp.where(mask, _log_keep(z), 0.0)
                 for z, mask in zip(zs, masks)]
    sums = [_split_dot(log_keep, u) for log_keep in log_keeps]
    state = list(state)
    for z, total, vt, c, mask in zip(zs, sums, vts, chain_of, masks):
        carry, acc = state[c]
        a = jnp.exp(z + total + carry)
        if mask is not None:
            a = jnp.where(mask, a, 0.0)
        state[c] = (carry + total[:, 0:1], acc + _dot_nt(a.astype(BF16), vt))
    return state


def _norm_proj_body(*refs, plan, has_w, has_wt, n_carried):
    x_ref, g_ref = refs[:2]
    rest = list(refs[2:])
    w_ref = rest.pop(0) if has_w else None
    wt_ref = rest.pop(0) if has_wt else None
    out_refs = rest[n_carried:]
    x = x_ref[...]
    ms = jnp.mean(x * x, axis=-1, keepdims=True)
    h = (x * lax.rsqrt(ms + RMS_EPS) * g_ref[...]).astype(BF16)
    for (transposed, lo, hi), writes in plan:
        p = _dot_nt(wt_ref[lo:hi, :], h) if transposed else _dot(h, w_ref[:, lo:hi])
        for oi, dlo, kind in writes:
            if kind == "f32":
                val = p
            elif kind == "bf16":
                val = p.astype(BF16)
            elif kind == "bf16_scaled":
                val = (p * Q_SCALE).astype(BF16)
            else:
                val = p * (1.0 / (1.0 + jnp.exp(-p)))
            if transposed:
                ref = out_refs[oi]
                ref[(0,) * (len(ref.shape) - 2)] = val
            else:
                out_refs[oi][:, dlo:dlo + (hi - lo)] = val


def _norm_proj(x2, gain, w, wt, outs, plan, tm, tiles_per_batch, layer=0, carried=None):
    n, d = x2.shape
    nt = n // tm
    in_specs = [pl.BlockSpec((tm, d), lambda i: (i, 0)), pl.BlockSpec((1, d), lambda i: (0, 0))]
    args = [x2, gain.reshape(1, d)]
    for m in (w, wt):
        if m is not None:
            in_specs.append(pl.BlockSpec(m.shape, lambda i: (0, 0)))
            args.append(m)
    carried = list(carried or [])
    aliases = {}
    out_specs, out_shape = [], []
    for kind, width, dt, *extra in outs:
        if kind == "rows":
            out_specs.append(pl.BlockSpec((tm, width), lambda i: (i, 0)))
            out_shape.append(jax.ShapeDtypeStruct((n, width), dt))
        elif kind == "t_batch":
            out_specs.append(pl.BlockSpec((1, width, tm),
                                          lambda i: (i // tiles_per_batch, 0, i % tiles_per_batch)))
            out_shape.append(jax.ShapeDtypeStruct((nt // tiles_per_batch, width, tiles_per_batch * tm), dt))
        elif kind == "t_layers":
            out_specs.append(pl.BlockSpec((1, 1, width, tm),
                                          lambda i: (layer, i // tiles_per_batch, 0, i % tiles_per_batch)))
            out_shape.append(jax.ShapeDtypeStruct(
                (extra[0], nt // tiles_per_batch, width, tiles_per_batch * tm), dt))
            if carried:
                aliases[len(args)] = len(out_shape) - 1
                in_specs.append(pl.BlockSpec(memory_space=pl.ANY))
                args.append(carried.pop(0))
        else:
            out_specs.append(pl.BlockSpec((1, width, tm), lambda i: (i, 0, 0)))
            out_shape.append(jax.ShapeDtypeStruct((nt, width, tm), dt))
    return pl.pallas_call(
        functools.partial(_norm_proj_body, plan=plan, has_w=w is not None, has_wt=wt is not None,
                          n_carried=len(aliases)),
        grid=(nt,),
        in_specs=in_specs,
        out_specs=out_specs,
        out_shape=out_shape,
        input_output_aliases=aliases,
        compiler_params=_cparams(1),
        name="norm_proj",
    )(*args)


_SEG = {}
_c = 0
for _name, _w in (("q_sb", D_SB), ("k_sb", D_SB), ("v_sb", D_SB), ("z_sb", D_SB),
                  ("q_mb", D_MB), ("k_mb", D_MB), ("v_mb", D_MB), ("z_mb", D_MB),
                  ("q_mem", D_MEM), ("z_mem", D_MEM)):
    _SEG[_name] = (_c, _c + _w)
    _c += _w
_T_NAMES = ("k_sb", "v_sb", "k_mb", "v_mb", "q_mb")


def _kv_weight_t(w_in_l):
    return jnp.concatenate([w_in_l[:, _SEG[s][0]:_SEG[s][1]] for s in _T_NAMES], axis=1).T.astype(BF16)


def _prompt_proj_plan(depth):
    t_rows = lambda i: (True, i * D_SB, (i + 1) * D_SB)
    outs = [("t_layers", D_SB, F32, depth)] * 4
    outs += [("t_tile", D_SB, BF16)] * 3 + [("t_tile", D_MB, F32)]
    outs += [("rows", D_MB, BF16), ("rows", D_SB, BF16), ("rows", D_MEM, BF16),
             ("rows", D_MIX, F32)]
    plan = [(t_rows(0), ((0, 0, "f32"), (4, 0, "bf16"))),
            (t_rows(1), ((1, 0, "f32"), (5, 0, "bf16"))),
            (t_rows(2), ((2, 0, "f32"),)),
            (t_rows(3), ((3, 0, "f32"), (6, 0, "bf16"))),
            (t_rows(4), ((7, 0, "f32"),))]
    plan += [((False,) + _SEG["k_mb"], ((8, 0, "bf16"),)),
             ((False,) + _SEG["q_sb"], ((9, 0, "bf16_scaled"),)),
             ((False,) + _SEG["q_mem"], ((10, 0, "bf16_scaled"),)),
             ((False,) + _SEG["z_sb"], ((11, 0, "silu"),)),
             ((False,) + _SEG["z_mb"], ((11, D_SB, "silu"),)),
             ((False,) + _SEG["z_mem"], ((11, D_SB + D_MB, "silu"),))]
    return outs, tuple(plan)


def _sample_proj_plan():
    outs = [("t_batch", D_SB, F32)] * 4
    outs += [("rows", D_MB, F32), ("rows", D_MB, F32)]
    outs += [("rows", D_SB, BF16), ("rows", D_MB, F32), ("rows", D_MEM, BF16),
             ("rows", D_MIX, F32)]
    plan = [((True, i * D_SB, (i + 1) * D_SB), ((i, 0, "f32"),)) for i in range(4)]
    plan += [((False,) + _SEG["k_mb"], ((4, 0, "f32"),)),
             ((False,) + _SEG["v_mb"], ((5, 0, "f32"),)),
             ((False,) + _SEG["q_sb"], ((6, 0, "bf16_scaled"),)),
             ((False,) + _SEG["q_mb"], ((7, 0, "f32"),)),
             ((False,) + _SEG["q_mem"], ((8, 0, "bf16_scaled"),)),
             ((False,) + _SEG["z_sb"], ((9, 0, "silu"),)),
             ((False,) + _SEG["z_mb"], ((9, D_SB, "silu"),)),
             ((False,) + _SEG["z_mem"], ((9, D_SB + D_MB, "silu"),))]
    return outs, tuple(plan)


def _mem_kv_plan():
    outs = [("t_tile", D_MEM, F32), ("t_tile", D_MEM, F32), ("t_tile", D_MEM, BF16), ("t_tile", D_MEM, BF16)]
    plan = (((True, 0, D_MEM), ((0, 0, "f32"), (2, 0, "bf16"))),
            ((True, D_MEM, 2 * D_MEM), ((1, 0, "f32"), (3, 0, "bf16"))))
    return outs, plan


def _merge_body(msb_ref, mmb_ref, mmem_ref, x_ref, w_ref, g_ref, o_ref):
    y = (_dot(msb_ref[...], w_ref[0:D_SB, :])
         + _dot(mmb_ref[...], w_ref[D_SB:D_SB + D_MB, :])
         + _dot(mmem_ref[...], w_ref[D_SB + D_MB:D_MIX, :]))
    ms = jnp.mean(y * y, axis=-1, keepdims=True)
    o_ref[...] = x_ref[...] + y * lax.rsqrt(ms + RMS_EPS) * g_ref[...]


def _merge(msb, mmb, mmem, x2, w_out_bf16, g_post, tm):
    n, d = x2.shape
    row = lambda w: pl.BlockSpec((tm, w), lambda i: (i, 0))
    return pl.pallas_call(
        _merge_body,
        grid=(n // tm,),
        in_specs=[row(D_SB), row(D_MB), row(D_MEM), row(d),
                  pl.BlockSpec((D_MIX, d), lambda i: (0, 0)),
                  pl.BlockSpec((1, d), lambda i: (0, 0))],
        out_specs=row(d),
        out_shape=jax.ShapeDtypeStruct((n, d), F32),
        compiler_params=_cparams(1),
        name="merge",
    )(msb, mmb, mmem, x2, w_out_bf16, g_post.reshape(1, d))


def _sb_prompt_body(q_ref, kt_ref, vt_ref, u_ref, g_ref, o_ref):
    tq = TOKEN_TILE
    n_batch = q_ref.shape[0]
    qi = pl.program_id(1)
    row = lax.broadcasted_iota(jnp.int32, (tq, tq), 0)
    col = lax.broadcasted_iota(jnp.int32, (tq, tq), 1)
    heads = [_head_lane_mask((tq, LANES), hh) for hh in range(PAIR)]
    chains = [(b, hh) for b in range(n_batch) for hh in range(PAIR)]
    qms = []
    for b, hh in chains:
        q = q_ref[b]
        qms.append(jnp.where(heads[hh], q, jnp.zeros_like(q)))

    def blocks(js, state, masks=None):
        return _sb_blocks(qms * len(js), [kt_ref[b, j] for j in js for b, _ in chains],
                          [vt_ref[b, j] for j in js for b, _ in chains], u_ref[...], state,
                          list(range(len(chains))) * len(js), masks)

    def carry_max(state):
        return jnp.max(functools.reduce(jnp.maximum, [carry for carry, _ in state]))

    diag = [col < row] * len(chains)
    zero = [(jnp.zeros((tq, 1), F32), jnp.zeros((tq, LANES), F32))] * len(chains)
    state = lax.cond(qi > 0, lambda: blocks([qi, qi - 1], zero, diag + [None] * len(chains)),
                     lambda: blocks([qi], zero, diag))

    def cond(loop):
        j, cmax, _ = loop
        return jnp.logical_and(j >= 0, cmax > -SB_EXIT)

    def body(loop):
        j, _, state = loop
        state = blocks([j], state)
        return j - 1, carry_max(state), state

    state = lax.while_loop(cond, body, (qi - 2, carry_max(state), state))[2]
    for b in range(n_batch):
        out = jnp.where(heads[0], state[PAIR * b][1], state[PAIR * b + 1][1])
        o_ref[b] = (out * g_ref[b]).astype(BF16)


def _prompt_specs(nq, g_off):
    tq = TOKEN_TILE
    tile = lambda off: pl.BlockSpec((tq, LANES), lambda b, p, i: (b * nq + i, p + off))
    keys = pl.BlockSpec((nq, LANES, tq), lambda b, p, i: (b, p, 0))
    return tile(0), keys, tile(g_off)


def _sb_prompt(qb, ktb, vtb, gates, u, batch, seq):
    tq = TOKEN_TILE
    nq = seq // tq
    tile = lambda: pl.BlockSpec((batch, tq, LANES), lambda p, i: (0, i, p))
    keys = pl.BlockSpec((batch, nq, LANES, tq), lambda p, i: (0, 0, p, 0))
    by_batch = lambda a: a.reshape((batch, a.shape[0] // batch) + a.shape[1:])
    return pl.pallas_call(
        _sb_prompt_body,
        grid=(D_SB // LANES, nq),
        in_specs=[tile(), keys, keys, pl.BlockSpec((tq, tq), lambda p, i: (0, 0)), tile()],
        out_specs=tile(),
        out_shape=jax.ShapeDtypeStruct((batch, seq, D_SB), BF16),
        compiler_params=_cparams(2),
        name="sb_prompt",
    )(by_batch(qb), by_batch(ktb), by_batch(vtb), u, by_batch(gates)).reshape(batch * seq, D_SB)


def _kmean_body(kt_ref, o_ref):
    nb = pl.program_id(1)
    mean = jnp.sum(kt_ref[0, 0], axis=1, keepdims=True) * (1.0 / MOBA_BLOCK)
    lane = lax.broadcasted_iota(jnp.int32, o_ref.shape[1:], 1)

    @pl.when(nb == 0)
    def _():
        o_ref[0] = jnp.zeros(o_ref.shape[1:], F32)

    o_ref[0] = jnp.where(lane == nb, mean, o_ref[0])


def _kmean(kt_f32, layer, batch, seq):
    nb = seq // MOBA_BLOCK
    assert nb <= LANES
    return pl.pallas_call(
        _kmean_body,
        grid=(batch, nb),
        in_specs=[pl.BlockSpec((1, 1, D_MB, MOBA_BLOCK), lambda b, n: (layer, b, 0, n))],
        out_specs=pl.BlockSpec((1, D_MB, LANES), lambda b, n: (b, 0, 0)),
        out_shape=jax.ShapeDtypeStruct((batch, D_MB, LANES), F32),
        compiler_params=_cparams(2),
        name="kmean",
    )(kt_f32)


def _t5_bucket_np(dist):
    n = np.maximum(dist, 0)
    nf = np.maximum(n, 1).astype(np.float32)
    large = MAX_EXACT + (np.log(nf / np.float32(MAX_EXACT)) / np.float32(math.log(MAX_DISTANCE / MAX_EXACT))
                         * np.float32(NUM_BUCKETS - MAX_EXACT)).astype(np.int32)
    large = np.minimum(large, NUM_BUCKETS - 1)
    return np.where(n < MAX_EXACT, n, large).astype(np.int32)


def _bias_tiles_body(tab_ref, bucket_ref, o_ref):
    h = pl.program_id(0)
    for t in range(2):
        bucket = bucket_ref[t]
        acc = jnp.zeros(bucket.shape, F32)
        for b in range(NUM_BUCKETS):
            acc = jnp.where(bucket == b, tab_ref[b, h], acc)
        o_ref[0, t] = acc - tab_ref[NUM_BUCKETS - 1, h]


def _bias_tiles(rel_bias):
    assert MOBA_BLOCK >= MAX_DISTANCE
    i = np.arange(MOBA_BLOCK)[None, :]
    j = np.arange(MOBA_BLOCK)[:, None]
    buckets = np.stack([_t5_bucket_np(i - j), _t5_bucket_np(MOBA_BLOCK + i - j)])
    return pl.pallas_call(
        _bias_tiles_body,
        grid=(H_MB,),
        in_specs=[pl.BlockSpec(memory_space=pltpu.SMEM),
                  pl.BlockSpec((2, MOBA_BLOCK, MOBA_BLOCK), lambda h: (0, 0, 0))],
        out_specs=pl.BlockSpec((1, 2, MOBA_BLOCK, MOBA_BLOCK), lambda h: (h, 0, 0, 0)),
        out_shape=jax.ShapeDtypeStruct((H_MB, 2, MOBA_BLOCK, MOBA_BLOCK), F32),
        compiler_params=_cparams(1),
        name="bias_tiles",
    )(rel_bias, jnp.asarray(buckets))


def _select_columns(gates):
    picked = []
    for n, g in enumerate(gates):
        beaten_by = [jnp.where((other > g) if m > n else (other >= g), 1.0, 0.0)
                     for m, other in enumerate(gates) if m != n]
        picked.append(functools.reduce(jnp.add, beaten_by) < MOBA_TOPK)
    return picked


def _select_blocks_t(gate, n_past):
    row = lax.broadcasted_iota(jnp.int32, gate.shape, 0)
    row_f = row.astype(F32)
    g = jnp.where(row < n_past, gate, NEG_INF)
    sel = jnp.zeros(gate.shape, F32)
    for _ in range(MOBA_TOPK):
        m = jnp.max(g, axis=0, keepdims=True)
        first = jnp.min(jnp.where(g == m, row_f, float(gate.shape[0])), axis=0, keepdims=True)
        pick = row_f == first
        sel = jnp.where(pick, 1.0, sel)
        g = jnp.where(pick, -jnp.inf, g)
    return jnp.where(row < n_past, sel, 0.0)


def _moba_prompt_body(qt_ref, k_ref, vt_ref, km_ref, bias_ref, g_ref, o_ref, pen_ref):
    c0 = pl.program_id(2)
    tq = MOBA_BLOCK
    n_rows = pen_ref.shape[1]
    qt = qt_ref[0]
    qtb = (qt * Q_SCALE).astype(BF16)
    km_t = km_ref[0].T[:n_rows]
    chan = lax.broadcasted_iota(jnp.int32, (LANES, tq), 0)
    key = lax.broadcasted_iota(jnp.int32, (tq, tq), 0)
    qry = lax.broadcasted_iota(jnp.int32, (tq, tq), 1)
    ones = jnp.ones((2 * SUBLANES, tq), BF16)
    prev = jnp.maximum(c0 - 1, 0)

    def keys_of(n):
        return k_ref[pl.ds(pl.multiple_of(n * tq, tq), tq), :]

    def values_of(hh, n):
        return jnp.concatenate([vt_ref[n][hh * HEAD_DIM:(hh + 1) * HEAD_DIM, :], ones], axis=0)

    def update(st, s, pen, vals):
        m, l, acc = st
        m_new = jnp.maximum(m, jnp.max(s, axis=0, keepdims=True) + pen)
        alpha = jnp.exp(m - m_new)
        pv = _dot(vals, jnp.exp(s - (m_new - pen)).astype(BF16))
        return m_new, alpha * l + pv[HEAD_DIM:HEAD_DIM + 1, :], alpha * acc + pv[:HEAD_DIM, :]

    heads = [(chan >= HEAD_DIM) if hh else (chan < HEAD_DIM) for hh in range(PAIR)]
    qts = [jnp.where(head, qtb, jnp.zeros_like(qtb)) for head in heads]
    gates = [jnp.dot(km_t, jnp.where(head, qt, 0.0), precision=lax.Precision.HIGHEST,
                     preferred_element_type=F32) for head in heads]
    s_own = [_dot(keys_of(c0), qts[hh]) for hh in range(PAIR)]
    s_prev = [_dot(keys_of(prev), qts[hh]) for hh in range(PAIR)]
    state = []
    for hh in range(PAIR):
        pen_ref[hh] = jnp.where(_select_blocks_t(gates[hh], c0) > 0.5, 0.0, NEG_INF)
        pen = pen_ref[hh, pl.ds(prev, 1), :]
        s_o = jnp.where(key <= qry, s_own[hh] + bias_ref[hh, 0], NEG_INF)
        s_p = s_prev[hh] + bias_ref[hh, 1]
        m = jnp.maximum(jnp.max(s_o, axis=0, keepdims=True), jnp.max(s_p, axis=0, keepdims=True) + pen)
        pv = (_dot(values_of(hh, c0), jnp.exp(s_o - m).astype(BF16))
              + _dot(values_of(hh, prev), jnp.exp(s_p - (m - pen)).astype(BF16)))
        state += [m, pv[HEAD_DIM:HEAD_DIM + 1, :], pv[:HEAD_DIM, :]]

    per_trip = MOBA_BLOCKS_PER_TRIP

    def body(i, st):
        blocks = []
        for k in range(per_trip):
            n = per_trip * i + k
            real = n < prev
            blocks.append((jnp.minimum(n, prev - 1), jnp.where(real, n, n_rows - 1)))
        scores = [[_dot(keys_of(n), qts[hh]) for hh in range(PAIR)] for n, _ in blocks]
        st = list(st)
        for (n, pen_row), sc in zip(blocks, scores):
            for hh in range(PAIR):
                st[3 * hh:3 * hh + 3] = update(st[3 * hh:3 * hh + 3], sc[hh],
                                               pen_ref[hh, pl.ds(pen_row, 1), :], values_of(hh, n))
        return tuple(st)

    final = lax.fori_loop(0, (prev + per_trip - 1) // per_trip, body, tuple(state))
    out_t = jnp.concatenate([final[2] / final[1], final[5] / final[4]], axis=0)
    o_ref[...] = (out_t.T * g_ref[...]).astype(BF16)


def _moba_prompt(qt, kb, vtb, kmean, bias, gates, batch, seq):
    tq = MOBA_BLOCK
    nq = seq // tq
    n_rows = -(-nq // SUBLANES) * SUBLANES
    tile, keys_t, gate_tile = _prompt_specs(nq, D_SB // LANES)
    return pl.pallas_call(
        _moba_prompt_body,
        grid=(batch, D_MB // LANES, nq),
        in_specs=[pl.BlockSpec((1, LANES, tq), lambda b, p, i: (b * nq + i, p, 0)),
                  pl.BlockSpec((seq, LANES), lambda b, p, i: (b, p)),
                  keys_t,
                  pl.BlockSpec((1, LANES, LANES), lambda b, p, i: (b, p, 0)),
                  pl.BlockSpec((PAIR, 2, tq, tq), lambda b, p, i: (p, 0, 0, 0)),
                  gate_tile],
        out_specs=tile,
        out_shape=jax.ShapeDtypeStruct((batch * seq, D_MB), BF16),
        scratch_shapes=[pltpu.VMEM((PAIR, n_rows, tq), F32)],
        compiler_params=_cparams(3),
        name="moba_prompt",
    )(qt, kb, vtb, kmean, bias, gates)


def _mem_prompt_body(q_ref, mkt_ref, mvt_ref, g_ref, o_ref):
    tq = q_ref.shape[0]
    for pr in range(D_MEM // LANES):
        chans = slice(pr * LANES, (pr + 1) * LANES)
        q = q_ref[:, chans]
        mkt = mkt_ref[0, chans, :]
        mvt = mvt_ref[0, chans, :]
        out = jnp.zeros((tq, LANES), F32)
        for hh in range(PAIR):
            head = _head_lane_mask((tq, LANES), hh)
            s = _dot(jnp.where(head, q, jnp.zeros_like(q)), mkt)
            p = jnp.exp(s - jnp.max(s, axis=1, keepdims=True))
            o = _dot_nt(p.astype(BF16), mvt) / jnp.sum(p, axis=1, keepdims=True)
            out = jnp.where(head, o, out)
        o_ref[:, chans] = (out * g_ref[:, chans]).astype(BF16)


def _mem_prompt(qb, mktb, mvtb, gates, batch, seq, tq):
    nq = seq // tq
    n_mem = mktb.shape[2]
    g_off = (D_SB + D_MB) // D_MEM
    mem = pl.BlockSpec((1, D_MEM, n_mem), lambda b, i: (b, 0, 0))
    return pl.pallas_call(
        _mem_prompt_body,
        grid=(batch, nq),
        in_specs=[pl.BlockSpec((tq, D_MEM), lambda b, i: (b * nq + i, 0)), mem, mem,
                  pl.BlockSpec((tq, D_MEM), lambda b, i: (b * nq + i, g_off))],
        out_specs=pl.BlockSpec((tq, D_MEM), lambda b, i: (b * nq + i, 0)),
        out_shape=jax.ShapeDtypeStruct((batch * seq, D_MEM), BF16),
        compiler_params=_cparams(2),
        name="mem_prompt",
    )(qb, mktb, mvtb, gates)


def _head_rows(x_row, width):
    r = lax.broadcasted_iota(jnp.int32, (SUBLANES, width), 0)
    lane = lax.broadcasted_iota(jnp.int32, (SUBLANES, width), 1)
    keep = (lane >= r * HEAD_DIM) & (lane < (r + 1) * HEAD_DIM)
    return jnp.where(keep, jnp.broadcast_to(x_row.astype(F32), (SUBLANES, width)), 0.0), keep


def _collapse_heads(x, keep):
    return jnp.sum(jnp.where(keep, x, 0.0), axis=0, keepdims=True)


def _sb_sample_chain(qm, u, k_refs, v_refs, carry, acc):
    n = len(k_refs)
    return _sb_blocks([qm] * n, [k_ref[0, 0].astype(BF16) for k_ref in reversed(k_refs)],
                      [v_ref[0, 0].astype(BF16) for v_ref in reversed(v_refs)], u, [(carry, acc)], [0] * n)[0]


def _sb_sample_head_body(pt_ref, q_ref, g_ref, u_ref, *refs, n_pages, group):
    del pt_ref
    n = n_pages * group
    k_refs, v_refs = refs[:n], refs[n:2 * n]
    o_ref, acc_ref, carry_ref, more_ref = refs[2 * n:]
    rows = [_head_rows(q_ref[g], D_SB) for g in range(group)]
    order = [(g, g * n_pages + p) for p in reversed(range(n_pages)) for g in range(group)]
    state = _sb_blocks([rows[g][0].astype(BF16) for g, _ in order],
                       [k_refs[i][0, 0].astype(BF16) for _, i in order],
                       [v_refs[i][0, 0].astype(BF16) for _, i in order], u_ref[...],
                       [(jnp.zeros((SUBLANES, 1), F32), jnp.zeros((SUBLANES, D_SB), F32))] * group,
                       [g for g, _ in order])
    live = lax.broadcasted_iota(jnp.int32, (SUBLANES, 1), 0) < H_SB
    for g, (carry, acc) in enumerate(state):
        o_ref[g] = (_collapse_heads(acc, rows[g][1]) * g_ref[g]).astype(BF16)
        acc_ref[g] = acc
        carry_ref[g] = jnp.broadcast_to(carry, (SUBLANES, LANES))
        more = jnp.max(jnp.where(live, carry, -jnp.inf)) > -SB_EXIT
        more_ref[g] = jnp.broadcast_to(more.astype(jnp.int32), (SUBLANES, LANES))


def _sb_sample_tail_body(pt_ref, more_ref, q_ref, g_ref, u_ref, acc_ref, carry_ref, *refs, n_pages):
    del pt_ref
    b = pl.program_id(0)
    k_refs, v_refs, o_ref = refs[:n_pages], refs[n_pages:2 * n_pages], refs[2 * n_pages]
    qm, keep = _head_rows(q_ref[0], D_SB)
    acc0 = acc_ref[0]

    def rest():
        return _sb_sample_chain(qm.astype(BF16), u_ref[...], k_refs, v_refs, carry_ref[0][:, 0:1], acc0)[1]

    acc = lax.cond(more_ref[b] != 0, rest, lambda: acc0)
    o_ref[0] = (_collapse_heads(acc, keep) * g_ref[0]).astype(BF16)


def _page_specs(layer, pages, width, gated=False):
    def spec(p):
        if gated:
            return pl.BlockSpec((1, 1, width, PAGE_SIZE),
                                lambda b, pt, more: (layer, jnp.where(more[b] != 0, pt[b, p], 0), 0, 0))
        return pl.BlockSpec((1, 1, width, PAGE_SIZE), lambda b, pt: (layer, pt[b, p], 0, 0))
    return [spec(p) for p in pages]


def _row_spec(width, off=0):
    return pl.BlockSpec((1, 1, width), lambda b, *_: (b, 0, off))


def _sb_sample(layer, page_table, qb, gates3, u, cache_kt, cache_vt):
    db, n_pages = page_table.shape
    group = SAMPLE_GROUP
    q3 = qb.reshape(db, 1, D_SB)
    head_pages = range(n_pages - SB_SAMPLE_HEAD_PAGES, n_pages)
    tail_pages = range(0, n_pages - SB_SAMPLE_HEAD_PAGES)
    u_spec = pl.BlockSpec((PAGE_SIZE, PAGE_SIZE), lambda b, *_: (0, 0))
    rows = lambda w, n=1: pl.BlockSpec((group, n, w), lambda b, *_: (b, 0, 0))
    group_pages = [pl.BlockSpec((1, 1, D_SB, PAGE_SIZE),
                                functools.partial(lambda b, pt, g, p: (layer, pt[b * group + g, p], 0, 0), g=g, p=p))
                   for g in range(group) for p in head_pages]
    n_head = len(group_pages)
    out, acc, carry, more = pl.pallas_call(
        functools.partial(_sb_sample_head_body, n_pages=len(head_pages), group=group),
        grid_spec=pltpu.PrefetchScalarGridSpec(
            num_scalar_prefetch=1,
            grid=(db // group,),
            in_specs=[rows(D_SB), rows(D_SB), u_spec] + group_pages + group_pages,
            out_specs=[rows(D_SB), rows(D_SB, SUBLANES), rows(LANES, SUBLANES), rows(LANES, SUBLANES)],
        ),
        out_shape=[jax.ShapeDtypeStruct((db, 1, D_SB), BF16),
                   jax.ShapeDtypeStruct((db, SUBLANES, D_SB), F32),
                   jax.ShapeDtypeStruct((db, SUBLANES, LANES), F32),
                   jax.ShapeDtypeStruct((db, SUBLANES, LANES), jnp.int32)],
        compiler_params=_cparams(1),
        name="sb_sample_head",
    )(page_table, q3, gates3, u, *([cache_kt] * n_head), *([cache_vt] * n_head))
    more = more[:, 0, 0]

    def tail():
        state = lambda w: pl.BlockSpec((1, SUBLANES, w), lambda b, *_: (b, 0, 0))
        return pl.pallas_call(
            functools.partial(_sb_sample_tail_body, n_pages=len(tail_pages)),
            grid_spec=pltpu.PrefetchScalarGridSpec(
                num_scalar_prefetch=2,
                grid=(db,),
                in_specs=[_row_spec(D_SB), _row_spec(D_SB), u_spec, state(D_SB), state(LANES)]
                         + _page_specs(layer, tail_pages, D_SB, gated=True)
                         + _page_specs(layer, tail_pages, D_SB, gated=True),
                out_specs=_row_spec(D_SB),
            ),
            out_shape=jax.ShapeDtypeStruct((db, 1, D_SB), BF16),
            compiler_params=_cparams(1),
            name="sb_sample_tail",
        )(page_table, more, q3, gates3, u, acc, carry,
          *([cache_kt] * len(tail_pages)), *([cache_vt] * len(tail_pages)))

    return lax.cond(jnp.any(more != 0), tail, lambda: out)


def _moba_sample_body(pt_ref, qf_ref, kn_ref, vn_ref, bias_ref, bown_ref, g_ref, *refs, n_pages):
    del pt_ref
    k_refs, v_refs, o_ref = refs[:n_pages], refs[n_pages:2 * n_pages], refs[2 * n_pages]
    pages_per_block = MOBA_BLOCK // PAGE_SIZE
    n_past = n_pages // pages_per_block
    qf, keep = _head_rows(qf_ref[0], D_MB)
    qs = qf * Q_SCALE
    q_hi = qs.astype(BF16).astype(F32)
    q_parts = jnp.concatenate([q_hi, qs - q_hi], axis=0).astype(BF16)

    scores = []
    for p in range(n_pages):
        both = _dot(q_parts, k_refs[p][0, 0].astype(BF16))
        scores.append(both[:SUBLANES] + both[SUBLANES:])
    gates = [jnp.sum(functools.reduce(jnp.add, scores[n * pages_per_block:(n + 1) * pages_per_block]),
                     axis=1, keepdims=True) * (1.0 / (Q_SCALE * MOBA_BLOCK)) for n in range(n_past)]
    sel = _select_columns(gates)

    kn = kn_ref[0].astype(BF16).astype(F32)
    s_own = jnp.sum(qf * Q_SCALE * kn, axis=1, keepdims=True) + bown_ref[:, 0:1]
    masked = [jnp.where(sel[p // pages_per_block], scores[p] + bias_ref[:, p * PAGE_SIZE:(p + 1) * PAGE_SIZE],
                        NEG_INF) for p in range(n_pages)]
    m = jnp.maximum(s_own, jnp.max(functools.reduce(jnp.maximum, masked), axis=1, keepdims=True))
    p_own = jnp.exp(s_own - m)
    weights = [jnp.exp(s - m) for s in masked]
    l = p_own + jnp.sum(functools.reduce(jnp.add, weights), axis=1, keepdims=True)
    acc = p_own * vn_ref[0]
    for w, v_ref in zip(weights, v_refs):
        acc = acc + _dot_nt(w.astype(BF16), v_ref[0, 0].astype(BF16))
    o_ref[0] = (_collapse_heads(acc / l, keep) * g_ref[0]).astype(BF16)


def _moba_sample(layer, page_table, qf, k_new, v_new, bias_keys, bias_own, gates3, cache_kt, cache_vt):
    db, n_pages = page_table.shape
    past = n_pages * PAGE_SIZE
    grid_spec = pltpu.PrefetchScalarGridSpec(
        num_scalar_prefetch=1,
        grid=(db,),
        in_specs=[_row_spec(D_MB), _row_spec(D_MB), _row_spec(D_MB),
                  pl.BlockSpec((SUBLANES, past), lambda b, pt: (0, 0)),
                  pl.BlockSpec((SUBLANES, LANES), lambda b, pt: (0, 0)),
                  _row_spec(D_MB, D_SB // D_MB)]
                 + _page_specs(layer, range(n_pages), D_MB) + _page_specs(layer, range(n_pages), D_MB),
        out_specs=_row_spec(D_MB),
    )
    r3 = lambda a: a.reshape(db, 1, D_MB)
    return pl.pallas_call(
        functools.partial(_moba_sample_body, n_pages=n_pages),
        grid_spec=grid_spec,
        out_shape=jax.ShapeDtypeStruct((db, 1, D_MB), BF16),
        compiler_params=_cparams(1),
        name="moba_sample",
    )(page_table, r3(qf), r3(k_new), r3(v_new), bias_keys, bias_own, gates3,
      *([cache_kt] * n_pages), *([cache_vt] * n_pages))


def _mem_sample_body(q_ref, mkt_ref, mvt_ref, g_ref, o_ref, *, group):
    rows = [_head_rows(q_ref[g], D_MEM) for g in range(group)]
    scores = [_dot(rows[g][0].astype(BF16), mkt_ref[0, g].astype(BF16)) for g in range(group)]
    probs = [jnp.exp(s - jnp.max(s, axis=1, keepdims=True)) for s in scores]
    outs = [_dot_nt(p.astype(BF16), mvt_ref[0, g].astype(BF16)) for g, p in enumerate(probs)]
    for g in range(group):
        o = outs[g] / jnp.sum(probs[g], axis=1, keepdims=True)
        o_ref[g] = (_collapse_heads(o, rows[g][1]) * g_ref[g]).astype(BF16)


def _mem_sample(layer, qb, gates3, mem_kt, mem_vt):
    db, n_mem = mem_kt.shape[1], mem_kt.shape[3]
    group = MEM_SAMPLE_GROUP
    mem = pl.BlockSpec((1, group, D_MEM, n_mem), lambda b: (layer, b, 0, 0))
    row = lambda off: pl.BlockSpec((group, 1, D_MEM), lambda b: (b, 0, off))
    return pl.pallas_call(
        functools.partial(_mem_sample_body, group=group),
        grid=(db // group,),
        in_specs=[row(0), mem, mem, row((D_SB + D_MB) // D_MEM)],
        out_specs=row(0),
        out_shape=jax.ShapeDtypeStruct((db, 1, D_MEM), BF16),
        compiler_params=_cparams(1),
        name="mem_sample",
    )(qb.reshape(db, 1, D_MEM), mem_kt, mem_vt, gates3)


def _upper_ones(n):
    i = np.arange(n)
    return jnp.asarray((i[:, None] >= i[None, :]).astype(np.float32), dtype=BF16)


def _channel_major(cache, heads):
    lead = cache.shape[:-3]
    t = jnp.moveaxis(cache, -3, -1)
    return t.reshape(lead + (heads * HEAD_DIM, cache.shape[-3]))


def _token_major(x, heads):
    lead = x.shape[:-2]
    return jnp.moveaxis(x.reshape(lead + (heads, HEAD_DIM, x.shape[-1])), -1, -3)


def kernel(x_prompt, x_sample, cache_sb_k, cache_sb_v, cache_moba_k, cache_moba_v, cache_mem_k,
           cache_mem_v, page_table, mem_prompt, rel_bias, g_pre, g_post, g_mem, w_in, w_out, w_mem_kv):
    batch, seq, d_model = x_prompt.shape
    db, dec_seq, _ = x_sample.shape
    depth = w_in.shape[0]
    n_mem = mem_prompt.shape[1]
    n_pages = page_table.shape[1]
    past = n_pages * PAGE_SIZE
    assert dec_seq == 1 and seq % TOKEN_TILE == 0 and past % MOBA_BLOCK == 0
    assert past >= 2 * MOBA_BLOCK and db % SAMPLE_GROUP == 0 and db % MEM_SAMPLE_GROUP == 0

    tm = TOKEN_TILE
    tiles = seq // tm
    u_block = _upper_ones(TOKEN_TILE)
    u_page = _upper_ones(PAGE_SIZE)
    p_outs, p_plan = _prompt_proj_plan(depth)
    s_outs, s_plan = _sample_proj_plan()
    kv_outs, kv_plan = _mem_kv_plan()

    bias = _bias_tiles(rel_bias)
    pad = jnp.zeros((SUBLANES - H_MB, past), F32)
    bias_keys = jnp.concatenate(
        [jnp.concatenate([jnp.zeros((H_MB, past - MOBA_BLOCK), F32), bias[:, 1, :, 0]], axis=1), pad], axis=0)
    bias_own = jnp.concatenate(
        [jnp.broadcast_to(bias[:, 0, 0, 0:1], (H_MB, LANES)), pad[:, :LANES]], axis=0)

    sb_kt, sb_vt = _channel_major(cache_sb_k, H_SB), _channel_major(cache_sb_v, H_SB)
    mb_kt, mb_vt = _channel_major(cache_moba_k, H_MB), _channel_major(cache_moba_v, H_MB)
    mem_kt, mem_vt = _channel_major(cache_mem_k, H_MEM), _channel_major(cache_mem_v, H_MEM)

    xp = x_prompt.reshape(batch * seq, d_model)
    xs = x_sample.reshape(db, d_model)
    mem2 = mem_prompt.reshape(batch * n_mem, d_model)
    pt = page_table.astype(jnp.int32)
    outs = [[] for _ in range(10)]
    kv_layers = None
    for l in range(depth):
        w_in_l = w_in[l].astype(BF16)
        w_kv_t = _kv_weight_t(w_in[l])
        w_out_l = w_out[l].astype(BF16)

        *kv_layers, ktb_sb, vtb_sb, vtb_mb, qt_mb, kb_mb, qb_sb, qb_mem, gates = _norm_proj(
            xp, g_pre[l], w_in_l, w_kv_t, p_outs, p_plan, tm, tiles, layer=l, carried=kv_layers)
        mkt, mvt, mktb, mvtb = _norm_proj(mem2, g_mem[l], None, w_mem_kv[l].T.astype(BF16),
                                          kv_outs, kv_plan, n_mem, 1)
        m_sb = _sb_prompt(qb_sb, ktb_sb, vtb_sb, gates, u_block, batch, seq)
        kmean = _kmean(kv_layers[2], l, batch, seq)
        m_mb = _moba_prompt(qt_mb, kb_mb, vtb_mb, kmean, bias, gates, batch, seq)
        m_mem = _mem_prompt(qb_mem, mktb, mvtb, gates, batch, seq, 2 * tm)
        xp = _merge(m_sb, m_mb, m_mem, xp, w_out_l, g_post[l], 2 * tm)
        outs[4].append(_token_major(mkt, H_MEM))
        outs[5].append(_token_major(mvt, H_MEM))

        (kt_sb, vt_sb, kt_mb, vt_mb, k_mb, v_mb, qb_sb, qf_mb, qb_mem, gates) = _norm_proj(
            xs, g_pre[l], w_in_l, w_kv_t, s_outs, s_plan, db, 1)
        gates3 = gates.reshape(db, 1, D_MIX)
        s_sb = _sb_sample(l, pt, qb_sb, gates3, u_page, sb_kt, sb_vt)
        s_mb = _moba_sample(l, pt, qf_mb, k_mb, v_mb, bias_keys, bias_own, gates3, mb_kt, mb_vt)
        s_mem = _mem_sample(l, qb_mem, gates3, mem_kt, mem_vt)
        xs = _merge(s_sb.reshape(db, D_SB), s_mb.reshape(db, D_MB), s_mem.reshape(db, D_MEM),
                    xs, w_out_l, g_post[l], db)
        for i, (a, h) in enumerate(((kt_sb, H_SB), (vt_sb, H_SB), (kt_mb, H_MB), (vt_mb, H_MB))):
            outs[6 + i].append(_token_major(a, h).reshape(db, 1, h, HEAD_DIM))

    prompt_kv = [_token_major(a, h) for a, h in zip(kv_layers, (H_SB, H_SB, H_MB, H_MB))]
    return ((xp.reshape(batch, seq, d_model), xs.reshape(db, 1, d_model)) + tuple(prompt_kv)
            + tuple(jnp.stack(o) for o in outs[4:]))
```

```python
import functools
import math

import numpy as np
import jax
import jax.numpy as jnp
from jax import lax
from jax.experimental import pallas as pl
from jax.experimental.pallas import tpu as pltpu

HEAD_DIM = 64
H_SB, H_MB, H_MEM = 6, 6, 4
D_SB, D_MB, D_MEM = H_SB * HEAD_DIM, H_MB * HEAD_DIM, H_MEM * HEAD_DIM
D_MIX = D_SB + D_MB + D_MEM
MOBA_BLOCK = 256
MOBA_TOPK = 3
PAGE_SIZE = 128
NUM_BUCKETS = 32
MAX_EXACT = 16
MAX_DISTANCE = 128
RMS_EPS = 1e-6
NEG_INF = -1e30
Q_SCALE = 1.0 / math.sqrt(HEAD_DIM)

LANES = 128
SUBLANES = 8
PAIR = LANES // HEAD_DIM
VMEM_LIMIT = 48 * 1024 * 1024
TOKEN_TILE = MOBA_BLOCK

SB_EXIT = 110.0
SB_SAMPLE_HEAD_PAGES = 2
MOBA_BLOCKS_PER_TRIP = 3
SAMPLE_GROUP = 4
MEM_SAMPLE_GROUP = 8

BF16 = jnp.bfloat16
F32 = jnp.float32
_NT = (((1,), (1,)), ((), ()))


def _cparams(n_axes):
    return pltpu.CompilerParams(dimension_semantics=("arbitrary",) * n_axes,
                                vmem_limit_bytes=VMEM_LIMIT)


def _dot(a, b):
    return jnp.dot(a, b, preferred_element_type=F32)


def _dot_nt(a, b):
    return lax.dot_general(a, b, _NT, preferred_element_type=F32)


def _split_bf16(x):
    hi = x.astype(BF16)
    return hi, (x - hi.astype(F32)).astype(BF16)


def _split_dot(x, u):
    hi, lo = _split_bf16(x)
    return _dot(hi, u) + _dot(lo, u)


def _head_lane_mask(shape, hh):
    lane = lax.broadcasted_iota(jnp.int32, shape, len(shape) - 1)
    return (lane >= HEAD_DIM) if hh else (lane < HEAD_DIM)


def _log_keep(z):
    return jnp.minimum(-z, 0.0) - jnp.log(1.0 + jnp.exp(-jnp.abs(z)))


def _sb_blocks(qms, kts, vts, u, state, chain_of, masks=None):
    masks = masks or [None] * len(qms)
    zs = [_dot(qm, kt) for qm, kt in zip(qms, kts)]
    log_keeps = [_log_keep(z) if mask is None else jnp.where(mask, _log_keep(z), 0.0)
                 for z, mask in zip(zs, masks)]
    sums = [_split_dot(log_keep, u) for log_keep in log_keeps]
    state = list(state)
    for z, total, vt, c, mask in zip(zs, sums, vts, chain_of, masks):
        carry, acc = state[c]
        a = jnp.exp(z + total + carry)
        if mask is not None:
            a = jnp.where(mask, a, 0.0)
        state[c] = (carry + total[:, 0:1], acc + _dot_nt(a.astype(BF16), vt))
    return state


def _norm_proj_body(*refs, plan, has_w, has_wt, n_carried):
    x_ref, g_ref = refs[:2]
    rest = list(refs[2:])
    w_ref = rest.pop(0) if has_w else None
    wt_ref = rest.pop(0) if has_wt else None
    out_refs = rest[n_carried:]
    x = x_ref[...]
    ms = jnp.mean(x * x, axis=-1, keepdims=True)
    h = (x * lax.rsqrt(ms + RMS_EPS) * g_ref[...]).astype(BF16)
    for (transposed, lo, hi), writes in plan:
        p = _dot_nt(wt_ref[lo:hi, :], h) if transposed else _dot(h, w_ref[:, lo:hi])
        for oi, dlo, kind in writes:
            if kind == "f32":
                val = p
            elif kind == "bf16":
                val = p.astype(BF16)
            elif kind == "bf16_scaled":
                val = (p * Q_SCALE).astype(BF16)
            else:
                val = p * (1.0 / (1.0 + jnp.exp(-p)))
            if transposed:
                ref = out_refs[oi]
                ref[(0,) * (len(ref.shape) - 2)] = val
            else:
                out_refs[oi][:, dlo:dlo + (hi - lo)] = val


def _norm_proj(x2, gain, w, wt, outs, plan, tm, tiles_per_batch, layer=0, carried=None):
    n, d = x2.shape
    nt = n // tm
    in_specs = [pl.BlockSpec((tm, d), lambda i: (i, 0)), pl.BlockSpec((1, d), lambda i: (0, 0))]
    args = [x2, gain.reshape(1, d)]
    for m in (w, wt):
        if m is not None:
            in_specs.append(pl.BlockSpec(m.shape, lambda i: (0, 0)))
            args.append(m)
    carried = list(carried or [])
    aliases = {}
    out_specs, out_shape = [], []
    for kind, width, dt, *extra in outs:
        if kind == "rows":
            out_specs.append(pl.BlockSpec((tm, width), lambda i: (i, 0)))
            out_shape.append(jax.ShapeDtypeStruct((n, width), dt))
        elif kind == "t_batch":
            out_specs.append(pl.BlockSpec((1, width, tm),
                                          lambda i: (i // tiles_per_batch, 0, i % tiles_per_batch)))
            out_shape.append(jax.ShapeDtypeStruct((nt // tiles_per_batch, width, tiles_per_batch * tm), dt))
        elif kind == "t_layers":
            out_specs.append(pl.BlockSpec((1, 1, width, tm),
                                          lambda i: (layer, i // tiles_per_batch, 0, i % tiles_per_batch)))
            out_shape.append(jax.ShapeDtypeStruct(
                (extra[0], nt // tiles_per_batch, width, tiles_per_batch * tm), dt))
            if carried:
                aliases[len(args)] = len(out_shape) - 1
                in_specs.append(pl.BlockSpec(memory_space=pl.ANY))
                args.append(carried.pop(0))
        else:
            out_specs.append(pl.BlockSpec((1, width, tm), lambda i: (i, 0, 0)))
            out_shape.append(jax.ShapeDtypeStruct((nt, width, tm), dt))
    return pl.pallas_call(
        functools.partial(_norm_proj_body, plan=plan, has_w=w is not None, has_wt=wt is not None,
                          n_carried=len(aliases)),
        grid=(nt,),
        in_specs=in_specs,
        out_specs=out_specs,
        out_shape=out_shape,
        input_output_aliases=aliases,
        compiler_params=_cparams(1),
        name="norm_proj",
    )(*args)


_SEG = {}
_c = 0
for _name, _w in (("q_sb", D_SB), ("k_sb", D_SB), ("v_sb", D_SB), ("z_sb", D_SB),
                  ("q_mb", D_MB), ("k_mb", D_MB), ("v_mb", D_MB), ("z_mb", D_MB),
                  ("q_mem", D_MEM), ("z_mem", D_MEM)):
    _SEG[_name] = (_c, _c + _w)
    _c += _w
_T_NAMES = ("k_sb", "v_sb", "k_mb", "v_mb", "q_mb")


def _kv_weight_t(w_in_l):
    return jnp.concatenate([w_in_l[:, _SEG[s][0]:_SEG[s][1]] for s in _T_NAMES], axis=1).T.astype(BF16)


def _prompt_proj_plan(depth):
    t_rows = lambda i: (True, i * D_SB, (i + 1) * D_SB)
    outs = [("t_layers", D_SB, F32, depth)] * 4
    outs += [("t_tile", D_SB, BF16)] * 3 + [("t_tile", D_MB, F32)]
    outs += [("rows", D_MB, BF16), ("rows", D_SB, BF16), ("rows", D_MEM, BF16),
             ("rows", D_MIX, F32)]
    plan = [(t_rows(0), ((0, 0, "f32"), (4, 0, "bf16"))),
            (t_rows(1), ((1, 0, "f32"), (5, 0, "bf16"))),
            (t_rows(2), ((2, 0, "f32"),)),
            (t_rows(3), ((3, 0, "f32"), (6, 0, "bf16"))),
            (t_rows(4), ((7, 0, "f32"),))]
    plan += [((False,) + _SEG["k_mb"], ((8, 0, "bf16"),)),
             ((False,) + _SEG["q_sb"], ((9, 0, "bf16_scaled"),)),
             ((False,) + _SEG["q_mem"], ((10, 0, "bf16_scaled"),)),
             ((False,) + _SEG["z_sb"], ((11, 0, "silu"),)),
             ((False,) + _SEG["z_mb"], ((11, D_SB, "silu"),)),
             ((False,) + _SEG["z_mem"], ((11, D_SB + D_MB, "silu"),))]
    return outs, tuple(plan)


def _sample_proj_plan():
    outs = [("t_batch", D_SB, F32)] * 4
    outs += [("rows", D_MB, F32), ("rows", D_MB, F32)]
    outs += [("rows", D_SB, BF16), ("rows", D_MB, F32), ("rows", D_MEM, BF16),
             ("rows", D_MIX, F32)]
    plan = [((True, i * D_SB, (i + 1) * D_SB), ((i, 0, "f32"),)) for i in range(4)]
    plan += [((False,) + _SEG["k_mb"], ((4, 0, "f32"),)),
             ((False,) + _SEG["v_mb"], ((5, 0, "f32"),)),
             ((False,) + _SEG["q_sb"], ((6, 0, "bf16_scaled"),)),
             ((False,) + _SEG["q_mb"], ((7, 0, "f32"),)),
             ((False,) + _SEG["q_mem"], ((8, 0, "bf16_scaled"),)),
             ((False,) + _SEG["z_sb"], ((9, 0, "silu"),)),
             ((False,) + _SEG["z_mb"], ((9, D_SB, "silu"),)),
             ((False,) + _SEG["z_mem"], ((9, D_SB + D_MB, "silu"),))]
    return outs, tuple(plan)


def _mem_kv_plan():
    outs = [("t_tile", D_MEM, F32), ("t_tile", D_MEM, F32), ("t_tile", D_MEM, BF16), ("t_tile", D_MEM, BF16)]
    plan = (((True, 0, D_MEM), ((0, 0, "f32"), (2, 0, "bf16"))),
            ((True, D_MEM, 2 * D_MEM), ((1, 0, "f32"), (3, 0, "bf16"))))
    return outs, plan


def _merge_body(msb_ref, mmb_ref, mmem_ref, x_ref, w_ref, g_ref, o_ref):
    y = (_dot(msb_ref[...], w_ref[0:D_SB, :])
         + _dot(mmb_ref[...], w_ref[D_SB:D_SB + D_MB, :])
         + _dot(mmem_ref[...], w_ref[D_SB + D_MB:D_MIX, :]))
    ms = jnp.mean(y * y, axis=-1, keepdims=True)
    o_ref[...] = x_ref[...] + y * lax.rsqrt(ms + RMS_EPS) * g_ref[...]


def _merge(msb, mmb, mmem, x2, w_out_bf16, g_post, tm):
    n, d = x2.shape
    row = lambda w: pl.BlockSpec((tm, w), lambda i: (i, 0))
    return pl.pallas_call(
        _merge_body,
        grid=(n // tm,),
        in_specs=[row(D_SB), row(D_MB), row(D_MEM), row(d),
                  pl.BlockSpec((D_MIX, d), lambda i: (0, 0)),
                  pl.BlockSpec((1, d), lambda i: (0, 0))],
        out_specs=row(d),
        out_shape=jax.ShapeDtypeStruct((n, d), F32),
        compiler_params=_cparams(1),
        name="merge",
    )(msb, mmb, mmem, x2, w_out_bf16, g_post.reshape(1, d))


def _sb_prompt_body(q_ref, kt_ref, vt_ref, u_ref, g_ref, o_ref):
    tq = TOKEN_TILE
    n_batch = q_ref.shape[0]
    qi = pl.program_id(1)
    row = lax.broadcasted_iota(jnp.int32, (tq, tq), 0)
    col = lax.broadcasted_iota(jnp.int32, (tq, tq), 1)
    heads = [_head_lane_mask((tq, LANES), hh) for hh in range(PAIR)]
    chains = [(b, hh) for b in range(n_batch) for hh in range(PAIR)]
    qms = []
    for b, hh in chains:
        q = q_ref[b]
        qms.append(jnp.where(heads[hh], q, jnp.zeros_like(q)))

    def blocks(js, state, masks=None):
        return _sb_blocks(qms * len(js), [kt_ref[b, j] for j in js for b, _ in chains],
                          [vt_ref[b, j] for j in js for b, _ in chains], u_ref[...], state,
                          list(range(len(chains))) * len(js), masks)

    def carry_max(state):
        return jnp.max(functools.reduce(jnp.maximum, [carry for carry, _ in state]))

    diag = [col < row] * len(chains)
    zero = [(jnp.zeros((tq, 1), F32), jnp.zeros((tq, LANES), F32))] * len(chains)
    state = lax.cond(qi > 0, lambda: blocks([qi, qi - 1], zero, diag + [None] * len(chains)),
                     lambda: blocks([qi], zero, diag))

    def cond(loop):
        j, cmax, _ = loop
        return jnp.logical_and(j >= 0, cmax > -SB_EXIT)

    def body(loop):
        j, _, state = loop
        state = blocks([j], state)
        return j - 1, carry_max(state), state

    state = lax.while_loop(cond, body, (qi - 2, carry_max(state), state))[2]
    for b in range(n_batch):
        out = jnp.where(heads[0], state[PAIR * b][1], state[PAIR * b + 1][1])
        o_ref[b] = (out * g_ref[b]).astype(BF16)


def _sb_prompt(qb, ktb, vtb, gates, u, batch, seq):
    tq = TOKEN_TILE
    nq = seq // tq
    tile = lambda: pl.BlockSpec((batch, tq, LANES), lambda p, i: (0, i, p))
    keys = pl.BlockSpec((batch, nq, LANES, tq), lambda p, i: (0, 0, p, 0))
    by_batch = lambda a: a.reshape((batch, a.shape[0] // batch) + a.shape[1:])
    return pl.pallas_call(
        _sb_prompt_body,
        grid=(D_SB // LANES, nq),
        in_specs=[tile(), keys, keys, pl.BlockSpec((tq, tq), lambda p, i: (0, 0)), tile()],
        out_specs=tile(),
        out_shape=jax.ShapeDtypeStruct((batch, seq, D_SB), BF16),
        compiler_params=_cparams(2),
        name="sb_prompt",
    )(by_batch(qb), by_batch(ktb), by_batch(vtb), u, by_batch(gates)).reshape(batch * seq, D_SB)


def _kmean_body(kt_ref, o_ref):
    nb = pl.program_id(1)
    mean = jnp.sum(kt_ref[0, 0], axis=1, keepdims=True) * (1.0 / MOBA_BLOCK)
    lane = lax.broadcasted_iota(jnp.int32, o_ref.shape[1:], 1)

    @pl.when(nb == 0)
    def _():
        o_ref[0] = jnp.zeros(o_ref.shape[1:], F32)

    o_ref[0] = jnp.where(lane == nb, mean, o_ref[0])


def _kmean(kt_f32, layer, batch, seq):
    nb = seq // MOBA_BLOCK
    assert nb <= LANES
    return pl.pallas_call(
        _kmean_body,
        grid=(batch, nb),
        in_specs=[pl.BlockSpec((1, 1, D_MB, MOBA_BLOCK), lambda b, n: (layer, b, 0, n))],
        out_specs=pl.BlockSpec((1, D_MB, LANES), lambda b, n: (b, 0, 0)),
        out_shape=jax.ShapeDtypeStruct((batch, D_MB, LANES), F32),
        compiler_params=_cparams(2),
        name="kmean",
    )(kt_f32)


def _t5_bucket_np(dist):
    n = np.maximum(dist, 0)
    nf = np.maximum(n, 1).astype(np.float32)
    large = MAX_EXACT + (np.log(nf / np.float32(MAX_EXACT)) / np.float32(math.log(MAX_DISTANCE / MAX_EXACT))
                         * np.float32(NUM_BUCKETS - MAX_EXACT)).astype(np.int32)
    large = np.minimum(large, NUM_BUCKETS - 1)
    return np.where(n < MAX_EXACT, n, large).astype(np.int32)


def _bias_tiles_body(tab_ref, bucket_ref, o_ref):
    h = pl.program_id(0)
    for t in range(2):
        bucket = bucket_ref[t]
        acc = jnp.zeros(bucket.shape, F32)
        for b in range(NUM_BUCKETS):
            acc = jnp.where(bucket == b, tab_ref[b, h], acc)
        o_ref[0, t] = acc - tab_ref[NUM_BUCKETS - 1, h]


def _bias_tiles(rel_bias):
    assert MOBA_BLOCK >= MAX_DISTANCE
    i = np.arange(MOBA_BLOCK)[None, :]
    j = np.arange(MOBA_BLOCK)[:, None]
    buckets = np.stack([_t5_bucket_np(i - j), _t5_bucket_np(MOBA_BLOCK + i - j)])
    return pl.pallas_call(
        _bias_tiles_body,
        grid=(H_MB,),
        in_specs=[pl.BlockSpec(memory_space=pltpu.SMEM),
                  pl.BlockSpec((2, MOBA_BLOCK, MOBA_BLOCK), lambda h: (0, 0, 0))],
        out_specs=pl.BlockSpec((1, 2, MOBA_BLOCK, MOBA_BLOCK), lambda h: (h, 0, 0, 0)),
        out_shape=jax.ShapeDtypeStruct((H_MB, 2, MOBA_BLOCK, MOBA_BLOCK), F32),
        compiler_params=_cparams(1),
        name="bias_tiles",
    )(rel_bias, jnp.asarray(buckets))


def _select_columns(gates):
    picked = []
    for n, g in enumerate(gates):
        beaten_by = [jnp.where((other > g) if m > n else (other >= g), 1.0, 0.0)
                     for m, other in enumerate(gates) if m != n]
        picked.append(functools.reduce(jnp.add, beaten_by) < MOBA_TOPK)
    return picked


def _select_blocks_t(gate, n_past):
    row = lax.broadcasted_iota(jnp.int32, gate.shape, 0)
    row_f = row.astype(F32)
    g = jnp.where(row < n_past, gate, NEG_INF)
    sel = jnp.zeros(gate.shape, F32)
    for _ in range(MOBA_TOPK):
        m = jnp.max(g, axis=0, keepdims=True)
        first = jnp.min(jnp.where(g == m, row_f, float(gate.shape[0])), axis=0, keepdims=True)
        pick = row_f == first
        sel = jnp.where(pick, 1.0, sel)
        g = jnp.where(pick, -jnp.inf, g)
    return jnp.where(row < n_past, sel, 0.0)


def _moba_prompt_body(qt_ref, k_ref, vt_ref, km_ref, bias_ref, g_ref, o_ref, pen_ref):
    c0 = pl.program_id(1)
    tq = MOBA_BLOCK
    n_batch = qt_ref.shape[0]
    n_rows = pen_ref.shape[1]
    chan = lax.broadcasted_iota(jnp.int32, (LANES, tq), 0)
    key = lax.broadcasted_iota(jnp.int32, (tq, tq), 0)
    qry = lax.broadcasted_iota(jnp.int32, (tq, tq), 1)
    ones = jnp.ones((2 * SUBLANES, tq), BF16)
    prev = jnp.maximum(c0 - 1, 0)
    heads = [(chan >= HEAD_DIM) if hh else (chan < HEAD_DIM) for hh in range(PAIR)]
    chains = [(b, hh) for b in range(n_batch) for hh in range(PAIR)]

    def keys_of(b, n):
        return k_ref[b, pl.ds(pl.multiple_of(n * tq, tq), tq), :]

    def values_of(b, hh, n):
        return jnp.concatenate([vt_ref[b, n][hh * HEAD_DIM:(hh + 1) * HEAD_DIM, :], ones], axis=0)

    def update(st, s, pen, vals):
        m, l, acc = st
        m_new = jnp.maximum(m, jnp.max(s, axis=0, keepdims=True) + pen)
        alpha = jnp.exp(m - m_new)
        pv = _dot(vals, jnp.exp(s - (m_new - pen)).astype(BF16))
        return m_new, alpha * l + pv[HEAD_DIM:HEAD_DIM + 1, :], alpha * acc + pv[:HEAD_DIM, :]

    qts, gates = [], []
    for b, hh in chains:
        qt = qt_ref[b, 0]
        qtb = (qt * Q_SCALE).astype(BF16)
        qts.append(jnp.where(heads[hh], qtb, jnp.zeros_like(qtb)))
        gates.append(jnp.dot(km_ref[b].T[:n_rows], jnp.where(heads[hh], qt, 0.0),
                             precision=lax.Precision.HIGHEST, preferred_element_type=F32))
    s_own = [_dot(keys_of(b, c0), qts[c]) for c, (b, _) in enumerate(chains)]
    s_prev = [_dot(keys_of(b, prev), qts[c]) for c, (b, _) in enumerate(chains)]
    state = []
    for c, (b, hh) in enumerate(chains):
        pen_ref[c] = jnp.where(_select_blocks_t(gates[c], c0) > 0.5, 0.0, NEG_INF)
        pen = pen_ref[c, pl.ds(prev, 1), :]
        s_o = jnp.where(key <= qry, s_own[c] + bias_ref[hh, 0], NEG_INF)
        s_p = s_prev[c] + bias_ref[hh, 1]
        m = jnp.maximum(jnp.max(s_o, axis=0, keepdims=True), jnp.max(s_p, axis=0, keepdims=True) + pen)
        pv = (_dot(values_of(b, hh, c0), jnp.exp(s_o - m).astype(BF16))
              + _dot(values_of(b, hh, prev), jnp.exp(s_p - (m - pen)).astype(BF16)))
        state += [m, pv[HEAD_DIM:HEAD_DIM + 1, :], pv[:HEAD_DIM, :]]

    per_trip = MOBA_BLOCKS_PER_TRIP

    def body(i, st):
        blocks = []
        for k in range(per_trip):
            n = per_trip * i + k
            real = n < prev
            blocks.append((jnp.minimum(n, prev - 1), jnp.where(real, n, n_rows - 1)))
        scores = [[_dot(keys_of(b, n), qts[c]) for c, (b, _) in enumerate(chains)]
                  for n, _ in blocks]
        st = list(st)
        for (n, pen_row), sc in zip(blocks, scores):
            for c, (b, hh) in enumerate(chains):
                st[3 * c:3 * c + 3] = update(st[3 * c:3 * c + 3], sc[c],
                                             pen_ref[c, pl.ds(pen_row, 1), :], values_of(b, hh, n))
        return tuple(st)

    final = lax.fori_loop(0, (prev + per_trip - 1) // per_trip, body, tuple(state))
    for b in range(n_batch):
        c = PAIR * b
        out_t = jnp.concatenate([final[3 * c + 2] / final[3 * c + 1], final[3 * c + 5] / final[3 * c + 4]],
                                axis=0)
        o_ref[b] = (out_t.T * g_ref[b]).astype(BF16)


def _moba_prompt(qt, kb, vtb, kmean, bias, gates, batch, seq):
    tq = MOBA_BLOCK
    nq = seq // tq
    n_rows = -(-nq // SUBLANES) * SUBLANES
    g_off = D_SB // LANES
    by_batch = lambda a: a.reshape((batch, a.shape[0] // batch) + a.shape[1:])
    return pl.pallas_call(
        _moba_prompt_body,
        grid=(D_MB // LANES, nq),
        in_specs=[pl.BlockSpec((batch, 1, LANES, tq), lambda p, i: (0, i, p, 0)),
                  pl.BlockSpec((batch, seq, LANES), lambda p, i: (0, 0, p)),
                  pl.BlockSpec((batch, nq, LANES, tq), lambda p, i: (0, 0, p, 0)),
                  pl.BlockSpec((batch, LANES, LANES), lambda p, i: (0, p, 0)),
                  pl.BlockSpec((PAIR, 2, tq, tq), lambda p, i: (p, 0, 0, 0)),
                  pl.BlockSpec((batch, tq, LANES), lambda p, i: (0, i, p + g_off))],
        out_specs=pl.BlockSpec((batch, tq, LANES), lambda p, i: (0, i, p)),
        out_shape=jax.ShapeDtypeStruct((batch, seq, D_MB), BF16),
        scratch_shapes=[pltpu.VMEM((batch * PAIR, n_rows, tq), F32)],
        compiler_params=_cparams(2),
        name="moba_prompt",
    )(by_batch(qt), by_batch(kb), by_batch(vtb), kmean, bias, by_batch(gates)).reshape(batch * seq, D_MB)


def _mem_prompt_body(q_ref, mkt_ref, mvt_ref, g_ref, o_ref):
    tq = q_ref.shape[0]
    for pr in range(D_MEM // LANES):
        chans = slice(pr * LANES, (pr + 1) * LANES)
        q = q_ref[:, chans]
        mkt = mkt_ref[0, chans, :]
        mvt = mvt_ref[0, chans, :]
        out = jnp.zeros((tq, LANES), F32)
        for hh in range(PAIR):
            head = _head_lane_mask((tq, LANES), hh)
            s = _dot(jnp.where(head, q, jnp.zeros_like(q)), mkt)
            p = jnp.exp(s - jnp.max(s, axis=1, keepdims=True))
            o = _dot_nt(p.astype(BF16), mvt) / jnp.sum(p, axis=1, keepdims=True)
            out = jnp.where(head, o, out)
        o_ref[:, chans] = (out * g_ref[:, chans]).astype(BF16)


def _mem_prompt(qb, mktb, mvtb, gates, batch, seq, tq):
    nq = seq // tq
    n_mem = mktb.shape[2]
    g_off = (D_SB + D_MB) // D_MEM
    mem = pl.BlockSpec((1, D_MEM, n_mem), lambda b, i: (b, 0, 0))
    return pl.pallas_call(
        _mem_prompt_body,
        grid=(batch, nq),
        in_specs=[pl.BlockSpec((tq, D_MEM), lambda b, i: (b * nq + i, 0)), mem, mem,
                  pl.BlockSpec((tq, D_MEM), lambda b, i: (b * nq + i, g_off))],
        out_specs=pl.BlockSpec((tq, D_MEM), lambda b, i: (b * nq + i, 0)),
        out_shape=jax.ShapeDtypeStruct((batch * seq, D_MEM), BF16),
        compiler_params=_cparams(2),
        name="mem_prompt",
    )(qb, mktb, mvtb, gates)


def _head_rows(x_row, width):
    r = lax.broadcasted_iota(jnp.int32, (SUBLANES, width), 0)
    lane = lax.broadcasted_iota(jnp.int32, (SUBLANES, width), 1)
    keep = (lane >= r * HEAD_DIM) & (lane < (r + 1) * HEAD_DIM)
    return jnp.where(keep, jnp.broadcast_to(x_row.astype(F32), (SUBLANES, width)), 0.0), keep


def _collapse_heads(x, keep):
    return jnp.sum(jnp.where(keep, x, 0.0), axis=0, keepdims=True)


def _sb_sample_chain(qm, u, k_refs, v_refs, carry, acc):
    n = len(k_refs)
    return _sb_blocks([qm] * n, [k_ref[0, 0].astype(BF16) for k_ref in reversed(k_refs)],
                      [v_ref[0, 0].astype(BF16) for v_ref in reversed(v_refs)], u, [(carry, acc)], [0] * n)[0]


def _sb_sample_head_body(pt_ref, q_ref, g_ref, u_ref, *refs, n_pages, group):
    del pt_ref
    n = n_pages * group
    k_refs, v_refs = refs[:n], refs[n:2 * n]
    o_ref, acc_ref, carry_ref, more_ref = refs[2 * n:]
    rows = [_head_rows(q_ref[g], D_SB) for g in range(group)]
    order = [(g, g * n_pages + p) for p in reversed(range(n_pages)) for g in range(group)]
    state = _sb_blocks([rows[g][0].astype(BF16) for g, _ in order],
                       [k_refs[i][0, 0].astype(BF16) for _, i in order],
                       [v_refs[i][0, 0].astype(BF16) for _, i in order], u_ref[...],
                       [(jnp.zeros((SUBLANES, 1), F32), jnp.zeros((SUBLANES, D_SB), F32))] * group,
                       [g for g, _ in order])
    live = lax.broadcasted_iota(jnp.int32, (SUBLANES, 1), 0) < H_SB
    for g, (carry, acc) in enumerate(state):
        o_ref[g] = (_collapse_heads(acc, rows[g][1]) * g_ref[g]).astype(BF16)
        acc_ref[g] = acc
        carry_ref[g] = jnp.broadcast_to(carry, (SUBLANES, LANES))
        more = jnp.max(jnp.where(live, carry, -jnp.inf)) > -SB_EXIT
        more_ref[g] = jnp.broadcast_to(more.astype(jnp.int32), (SUBLANES, LANES))


def _sb_sample_tail_body(pt_ref, more_ref, q_ref, g_ref, u_ref, acc_ref, carry_ref, *refs, n_pages):
    del pt_ref
    b = pl.program_id(0)
    k_refs, v_refs, o_ref = refs[:n_pages], refs[n_pages:2 * n_pages], refs[2 * n_pages]
    qm, keep = _head_rows(q_ref[0], D_SB)
    acc0 = acc_ref[0]

    def rest():
        return _sb_sample_chain(qm.astype(BF16), u_ref[...], k_refs, v_refs, carry_ref[0][:, 0:1], acc0)[1]

    acc = lax.cond(more_ref[b] != 0, rest, lambda: acc0)
    o_ref[0] = (_collapse_heads(acc, keep) * g_ref[0]).astype(BF16)


def _page_specs(layer, pages, width, gated=False):
    def spec(p):
        if gated:
            return pl.BlockSpec((1, 1, width, PAGE_SIZE),
                                lambda b, pt, more: (layer, jnp.where(more[b] != 0, pt[b, p], 0), 0, 0))
        return pl.BlockSpec((1, 1, width, PAGE_SIZE), lambda b, pt: (layer, pt[b, p], 0, 0))
    return [spec(p) for p in pages]


def _row_spec(width, off=0):
    return pl.BlockSpec((1, 1, width), lambda b, *_: (b, 0, off))


def _sb_sample(layer, page_table, qb, gates3, u, cache_kt, cache_vt):
    db, n_pages = page_table.shape
    group = SAMPLE_GROUP
    q3 = qb.reshape(db, 1, D_SB)
    head_pages = range(n_pages - SB_SAMPLE_HEAD_PAGES, n_pages)
    tail_pages = range(0, n_pages - SB_SAMPLE_HEAD_PAGES)
    u_spec = pl.BlockSpec((PAGE_SIZE, PAGE_SIZE), lambda b, *_: (0, 0))
    rows = lambda w, n=1: pl.BlockSpec((group, n, w), lambda b, *_: (b, 0, 0))
    group_pages = [pl.BlockSpec((1, 1, D_SB, PAGE_SIZE),
                                functools.partial(lambda b, pt, g, p: (layer, pt[b * group + g, p], 0, 0), g=g, p=p))
                   for g in range(group) for p in head_pages]
    n_head = len(group_pages)
    out, acc, carry, more = pl.pallas_call(
        functools.partial(_sb_sample_head_body, n_pages=len(head_pages), group=group),
        grid_spec=pltpu.PrefetchScalarGridSpec(
            num_scalar_prefetch=1,
            grid=(db // group,),
            in_specs=[rows(D_SB), rows(D_SB), u_spec] + group_pages + group_pages,
            out_specs=[rows(D_SB), rows(D_SB, SUBLANES), rows(LANES, SUBLANES), rows(LANES, SUBLANES)],
        ),
        out_shape=[jax.ShapeDtypeStruct((db, 1, D_SB), BF16),
                   jax.ShapeDtypeStruct((db, SUBLANES, D_SB), F32),
                   jax.ShapeDtypeStruct((db, SUBLANES, LANES), F32),
                   jax.ShapeDtypeStruct((db, SUBLANES, LANES), jnp.int32)],
        compiler_params=_cparams(1),
        name="sb_sample_head",
    )(page_table, q3, gates3, u, *([cache_kt] * n_head), *([cache_vt] * n_head))
    more = more[:, 0, 0]

    def tail():
        state = lambda w: pl.BlockSpec((1, SUBLANES, w), lambda b, *_: (b, 0, 0))
        return pl.pallas_call(
            functools.partial(_sb_sample_tail_body, n_pages=len(tail_pages)),
            grid_spec=pltpu.PrefetchScalarGridSpec(
                num_scalar_prefetch=2,
                grid=(db,),
                in_specs=[_row_spec(D_SB), _row_spec(D_SB), u_spec, state(D_SB), state(LANES)]
                         + _page_specs(layer, tail_pages, D_SB, gated=True)
                         + _page_specs(layer, tail_pages, D_SB, gated=True),
                out_specs=_row_spec(D_SB),
            ),
            out_shape=jax.ShapeDtypeStruct((db, 1, D_SB), BF16),
            compiler_params=_cparams(1),
            name="sb_sample_tail",
        )(page_table, more, q3, gates3, u, acc, carry,
          *([cache_kt] * len(tail_pages)), *([cache_vt] * len(tail_pages)))

    return lax.cond(jnp.any(more != 0), tail, lambda: out)


def _moba_sample_body(pt_ref, qf_ref, kn_ref, vn_ref, bias_ref, bown_ref, g_ref, *refs, n_pages):
    del pt_ref
    k_refs, v_refs, o_ref = refs[:n_pages], refs[n_pages:2 * n_pages], refs[2 * n_pages]
    pages_per_block = MOBA_BLOCK // PAGE_SIZE
    n_past = n_pages // pages_per_block
    qf, keep = _head_rows(qf_ref[0], D_MB)
    qs = qf * Q_SCALE
    q_hi = qs.astype(BF16).astype(F32)
    q_parts = jnp.concatenate([q_hi, qs - q_hi], axis=0).astype(BF16)

    scores = []
    for p in range(n_pages):
        both = _dot(q_parts, k_refs[p][0, 0].astype(BF16))
        scores.append(both[:SUBLANES] + both[SUBLANES:])
    gates = [jnp.sum(functools.reduce(jnp.add, scores[n * pages_per_block:(n + 1) * pages_per_block]),
                     axis=1, keepdims=True) * (1.0 / (Q_SCALE * MOBA_BLOCK)) for n in range(n_past)]
    sel = _select_columns(gates)

    kn = kn_ref[0].astype(BF16).astype(F32)
    s_own = jnp.sum(qf * Q_SCALE * kn, axis=1, keepdims=True) + bown_ref[:, 0:1]
    masked = [jnp.where(sel[p // pages_per_block], scores[p] + bias_ref[:, p * PAGE_SIZE:(p + 1) * PAGE_SIZE],
                        NEG_INF) for p in range(n_pages)]
    m = jnp.maximum(s_own, jnp.max(functools.reduce(jnp.maximum, masked), axis=1, keepdims=True))
    p_own = jnp.exp(s_own - m)
    weights = [jnp.exp(s - m) for s in masked]
    l = p_own + jnp.sum(functools.reduce(jnp.add, weights), axis=1, keepdims=True)
    acc = p_own * vn_ref[0]
    for w, v_ref in zip(weights, v_refs):
        acc = acc + _dot_nt(w.astype(BF16), v_ref[0, 0].astype(BF16))
    o_ref[0] = (_collapse_heads(acc / l, keep) * g_ref[0]).astype(BF16)


def _moba_sample(layer, page_table, qf, k_new, v_new, bias_keys, bias_own, gates3, cache_kt, cache_vt):
    db, n_pages = page_table.shape
    past = n_pages * PAGE_SIZE
    grid_spec = pltpu.PrefetchScalarGridSpec(
        num_scalar_prefetch=1,
        grid=(db,),
        in_specs=[_row_spec(D_MB), _row_spec(D_MB), _row_spec(D_MB),
                  pl.BlockSpec((SUBLANES, past), lambda b, pt: (0, 0)),
                  pl.BlockSpec((SUBLANES, LANES), lambda b, pt: (0, 0)),
                  _row_spec(D_MB, D_SB // D_MB)]
                 + _page_specs(layer, range(n_pages), D_MB) + _page_specs(layer, range(n_pages), D_MB),
        out_specs=_row_spec(D_MB),
    )
    r3 = lambda a: a.reshape(db, 1, D_MB)
    return pl.pallas_call(
        functools.partial(_moba_sample_body, n_pages=n_pages),
        grid_spec=grid_spec,
        out_shape=jax.ShapeDtypeStruct((db, 1, D_MB), BF16),
        compiler_params=_cparams(1),
        name="moba_sample",
    )(page_table, r3(qf), r3(k_new), r3(v_new), bias_keys, bias_own, gates3,
      *([cache_kt] * n_pages), *([cache_vt] * n_pages))


def _mem_sample_body(q_ref, mkt_ref, mvt_ref, g_ref, o_ref, *, group):
    rows = [_head_rows(q_ref[g], D_MEM) for g in range(group)]
    scores = [_dot(rows[g][0].astype(BF16), mkt_ref[0, g].astype(BF16)) for g in range(group)]
    probs = [jnp.exp(s - jnp.max(s, axis=1, keepdims=True)) for s in scores]
    outs = [_dot_nt(p.astype(BF16), mvt_ref[0, g].astype(BF16)) for g, p in enumerate(probs)]
    for g in range(group):
        o = outs[g] / jnp.sum(probs[g], axis=1, keepdims=True)
        o_ref[g] = (_collapse_heads(o, rows[g][1]) * g_ref[g]).astype(BF16)


def _mem_sample(layer, qb, gates3, mem_kt, mem_vt):
    db, n_mem = mem_kt.shape[1], mem_kt.shape[3]
    group = MEM_SAMPLE_GROUP
    mem = pl.BlockSpec((1, group, D_MEM, n_mem), lambda b: (layer, b, 0, 0))
    row = lambda off: pl.BlockSpec((group, 1, D_MEM), lambda b: (b, 0, off))
    return pl.pallas_call(
        functools.partial(_mem_sample_body, group=group),
        grid=(db // group,),
        in_specs=[row(0), mem, mem, row((D_SB + D_MB) // D_MEM)],
        out_specs=row(0),
        out_shape=jax.ShapeDtypeStruct((db, 1, D_MEM), BF16),
        compiler_params=_cparams(1),
        name="mem_sample",
    )(qb.reshape(db, 1, D_MEM), mem_kt, mem_vt, gates3)


def _upper_ones(n):
    i = np.arange(n)
    return jnp.asarray((i[:, None] >= i[None, :]).astype(np.float32), dtype=BF16)


def _channel_major(cache, heads):
    lead = cache.shape[:-3]
    t = jnp.moveaxis(cache, -3, -1)
    return t.reshape(lead + (heads * HEAD_DIM, cache.shape[-3]))


def _token_major(x, heads):
    lead = x.shape[:-2]
    return jnp.moveaxis(x.reshape(lead + (heads, HEAD_DIM, x.shape[-1])), -1, -3)


def kernel(x_prompt, x_sample, cache_sb_k, cache_sb_v, cache_moba_k, cache_moba_v, cache_mem_k,
           cache_mem_v, page_table, mem_prompt, rel_bias, g_pre, g_post, g_mem, w_in, w_out, w_mem_kv):
    batch, seq, d_model = x_prompt.shape
    db, dec_seq, _ = x_sample.shape
    depth = w_in.shape[0]
    n_mem = mem_prompt.shape[1]
    n_pages = page_table.shape[1]
    past = n_pages * PAGE_SIZE
    assert dec_seq == 1 and seq % TOKEN_TILE == 0 and past % MOBA_BLOCK == 0
    assert past >= 2 * MOBA_BLOCK and db % SAMPLE_GROUP == 0 and db % MEM_SAMPLE_GROUP == 0

    tm = TOKEN_TILE
    tiles = seq // tm
    u_block = _upper_ones(TOKEN_TILE)
    u_page = _upper_ones(PAGE_SIZE)
    p_outs, p_plan = _prompt_proj_plan(depth)
    s_outs, s_plan = _sample_proj_plan()
    kv_outs, kv_plan = _mem_kv_plan()

    bias = _bias_tiles(rel_bias)
    pad = jnp.zeros((SUBLANES - H_MB, past), F32)
    bias_keys = jnp.concatenate(
        [jnp.concatenate([jnp.zeros((H_MB, past - MOBA_BLOCK), F32), bias[:, 1, :, 0]], axis=1), pad], axis=0)
    bias_own = jnp.concatenate(
        [jnp.broadcast_to(bias[:, 0, 0, 0:1], (H_MB, LANES)), pad[:, :LANES]], axis=0)

    sb_kt, sb_vt = _channel_major(cache_sb_k, H_SB), _channel_major(cache_sb_v, H_SB)
    mb_kt, mb_vt = _channel_major(cache_moba_k, H_MB), _channel_major(cache_moba_v, H_MB)
    mem_kt, mem_vt = _channel_major(cache_mem_k, H_MEM), _channel_major(cache_mem_v, H_MEM)

    xp = x_prompt.reshape(batch * seq, d_model)
    xs = x_sample.reshape(db, d_model)
    mem2 = mem_prompt.reshape(batch * n_mem, d_model)
    pt = page_table.astype(jnp.int32)
    outs = [[] for _ in range(10)]
    kv_layers = None
    for l in range(depth):
        w_in_l = w_in[l].astype(BF16)
        w_kv_t = _kv_weight_t(w_in[l])
        w_out_l = w_out[l].astype(BF16)

        *kv_layers, ktb_sb, vtb_sb, vtb_mb, qt_mb, kb_mb, qb_sb, qb_mem, gates = _norm_proj(
            xp, g_pre[l], w_in_l, w_kv_t, p_outs, p_plan, tm, tiles, layer=l, carried=kv_layers)
        mkt, mvt, mktb, mvtb = _norm_proj(mem2, g_mem[l], None, w_mem_kv[l].T.astype(BF16),
                                          kv_outs, kv_plan, n_mem, 1)
        m_sb = _sb_prompt(qb_sb, ktb_sb, vtb_sb, gates, u_block, batch, seq)
        kmean = _kmean(kv_layers[2], l, batch, seq)
        m_mb = _moba_prompt(qt_mb, kb_mb, vtb_mb, kmean, bias, gates, batch, seq)
        m_mem = _mem_prompt(qb_mem, mktb, mvtb, gates, batch, seq, 2 * tm)
        xp = _merge(m_sb, m_mb, m_mem, xp, w_out_l, g_post[l], 2 * tm)
        outs[4].append(_token_major(mkt, H_MEM))
        outs[5].append(_token_major(mvt, H_MEM))

        (kt_sb, vt_sb, kt_mb, vt_mb, k_mb, v_mb, qb_sb, qf_mb, qb_mem, gates) = _norm_proj(
            xs, g_pre[l], w_in_l, w_kv_t, s_outs, s_plan, db, 1)
        gates3 = gates.reshape(db, 1, D_MIX)
        s_sb = _sb_sample(l, pt, qb_sb, gates3, u_page, sb_kt, sb_vt)
        s_mb = _moba_sample(l, pt, qf_mb, k_mb, v_mb, bias_keys, bias_own, gates3, mb_kt, mb_vt)
        s_mem = _mem_sample(l, qb_mem, gates3, mem_kt, mem_vt)
        xs = _merge(s_sb.reshape(db, D_SB), s_mb.reshape(db, D_MB), s_mem.reshape(db, D_MEM),
                    xs, w_out_l, g_post[l], db)
        for i, (a, h) in enumerate(((kt_sb, H_SB), (vt_sb, H_SB), (kt_mb, H_MB), (vt_mb, H_MB))):
            outs[6 + i].append(_token_major(a, h).reshape(db, 1, h, HEAD_DIM))

    prompt_kv = [_token_major(a, h) for a, h in zip(kv_layers, (H_SB, H_SB, H_MB, H_MB))]
    return ((xp.reshape(batch, seq, d_model), xs.reshape(db, 1, d_model)) + tuple(prompt_kv)
            + tuple(jnp.stack(o) for o in outs[4:]))
```

```python
import functools
import math

import numpy as np
import jax
import jax.numpy as jnp
from jax import lax
from jax.experimental import pallas as pl
from jax.experimental.pallas import tpu as pltpu

HEAD_DIM = 64
H_SB, H_MB, H_MEM = 6, 6, 4
D_SB, D_MB, D_MEM = H_SB * HEAD_DIM, H_MB * HEAD_DIM, H_MEM * HEAD_DIM
D_MIX = D_SB + D_MB + D_MEM
MOBA_BLOCK = 256
MOBA_TOPK = 3
PAGE_SIZE = 128
NUM_BUCKETS = 32
MAX_EXACT = 16
MAX_DISTANCE = 128
RMS_EPS = 1e-6
NEG_INF = -1e30
Q_SCALE = 1.0 / math.sqrt(HEAD_DIM)

LANES = 128
SUBLANES = 8
PAIR = LANES // HEAD_DIM
VMEM_LIMIT = 48 * 1024 * 1024
TOKEN_TILE = MOBA_BLOCK

SB_EXIT = 110.0
SB_SAMPLE_HEAD_PAGES = 2
MOBA_BLOCKS_PER_TRIP = 3
SAMPLE_GROUP = 8
MEM_SAMPLE_GROUP = 8

BF16 = jnp.bfloat16
F32 = jnp.float32
_NT = (((1,), (1,)), ((), ()))


def _cparams(n_axes):
    return pltpu.CompilerParams(dimension_semantics=("arbitrary",) * n_axes,
                                vmem_limit_bytes=VMEM_LIMIT)


def _dot(a, b):
    return jnp.dot(a, b, preferred_element_type=F32)


def _dot_nt(a, b):
    return lax.dot_general(a, b, _NT, preferred_element_type=F32)


def _split_bf16(x):
    hi = x.astype(BF16)
    return hi, (x - hi.astype(F32)).astype(BF16)


def _split_dot(x, u):
    hi, lo = _split_bf16(x)
    return _dot(hi, u) + _dot(lo, u)


def _head_lane_mask(shape, hh):
    lane = lax.broadcasted_iota(jnp.int32, shape, len(shape) - 1)
    return (lane >= HEAD_DIM) if hh else (lane < HEAD_DIM)


def _log_keep(z):
    return jnp.minimum(-z, 0.0) - jnp.log(1.0 + jnp.exp(-jnp.abs(z)))


def _sb_blocks(qms, kts, vts, u, state, chain_of, masks=None):
    masks = masks or [None] * len(qms)
    zs = [_dot(qm, kt) for qm, kt in zip(qms, kts)]
    log_keeps = [_log_keep(z) if mask is None else jnp.where(mask, _log_keep(z), 0.0)
                 for z, mask in zip(zs, masks)]
    sums = [_split_dot(log_keep, u) for log_keep in log_keeps]
    state = list(state)
    for z, total, vt, c, mask in zip(zs, sums, vts, chain_of, masks):
        carry, acc = state[c]
        a = jnp.exp(z + total + carry)
        if mask is not None:
            a = jnp.where(mask, a, 0.0)
        state[c] = (carry + total[:, 0:1], acc + _dot_nt(a.astype(BF16), vt))
    return state


def _norm_proj_body(*refs, plan, has_w, has_wt, n_carried, tiles_per_batch):
    x_ref, g_ref = refs[:2]
    rest = list(refs[2:])
    w_ref = rest.pop(0) if has_w else None
    wt_ref = rest.pop(0) if has_wt else None
    out_refs = rest[n_carried:]
    x = x_ref[...]
    ms = jnp.mean(x * x, axis=-1, keepdims=True)
    h = (x * lax.rsqrt(ms + RMS_EPS) * g_ref[...]).astype(BF16)
    for (transposed, lo, hi), writes in plan:
        p = _dot_nt(wt_ref[lo:hi, :], h) if transposed else _dot(h, w_ref[:, lo:hi])
        for oi, dlo, kind in writes:
            if kind == "block_mean":
                ref = out_refs[oi]
                t = pl.program_id(0) % tiles_per_batch
                lane = lax.broadcasted_iota(jnp.int32, ref.shape[1:], 1)
                mean = jnp.sum(p, axis=1, keepdims=True) * (1.0 / p.shape[1])

                @pl.when(t == 0)
                def _(ref=ref):
                    ref[0] = jnp.zeros(ref.shape[1:], F32)

                ref[0] = jnp.where(lane == t, mean, ref[0])
                continue
            if kind == "f32":
                val = p
            elif kind == "bf16":
                val = p.astype(BF16)
            elif kind == "bf16_scaled":
                val = (p * Q_SCALE).astype(BF16)
            else:
                val = p * (1.0 / (1.0 + jnp.exp(-p)))
            if transposed:
                ref = out_refs[oi]
                ref[(0,) * (len(ref.shape) - 2)] = val
            else:
                out_refs[oi][:, dlo:dlo + (hi - lo)] = val


def _norm_proj(x2, gain, w, wt, outs, plan, tm, tiles_per_batch, layer=0, carried=None):
    n, d = x2.shape
    nt = n // tm
    in_specs = [pl.BlockSpec((tm, d), lambda i: (i, 0)), pl.BlockSpec((1, d), lambda i: (0, 0))]
    args = [x2, gain.reshape(1, d)]
    for m in (w, wt):
        if m is not None:
            in_specs.append(pl.BlockSpec(m.shape, lambda i: (0, 0)))
            args.append(m)
    carried = list(carried or [])
    aliases = {}
    out_specs, out_shape = [], []
    for kind, width, dt, *extra in outs:
        if kind == "rows":
            out_specs.append(pl.BlockSpec((tm, width), lambda i: (i, 0)))
            out_shape.append(jax.ShapeDtypeStruct((n, width), dt))
        elif kind == "t_batch":
            out_specs.append(pl.BlockSpec((1, width, tm),
                                          lambda i: (i // tiles_per_batch, 0, i % tiles_per_batch)))
            out_shape.append(jax.ShapeDtypeStruct((nt // tiles_per_batch, width, tiles_per_batch * tm), dt))
        elif kind == "block_means":
            assert tiles_per_batch <= LANES
            out_specs.append(pl.BlockSpec((1, width, LANES), lambda i: (i // tiles_per_batch, 0, 0)))
            out_shape.append(jax.ShapeDtypeStruct((nt // tiles_per_batch, width, LANES), dt))
        elif kind == "t_layers":
            out_specs.append(pl.BlockSpec((1, 1, width, tm),
                                          lambda i: (layer, i // tiles_per_batch, 0, i % tiles_per_batch)))
            out_shape.append(jax.ShapeDtypeStruct(
                (extra[0], nt // tiles_per_batch, width, tiles_per_batch * tm), dt))
            if carried:
                aliases[len(args)] = len(out_shape) - 1
                in_specs.append(pl.BlockSpec(memory_space=pl.ANY))
                args.append(carried.pop(0))
        else:
            out_specs.append(pl.BlockSpec((1, width, tm), lambda i: (i, 0, 0)))
            out_shape.append(jax.ShapeDtypeStruct((nt, width, tm), dt))
    return pl.pallas_call(
        functools.partial(_norm_proj_body, plan=plan, has_w=w is not None, has_wt=wt is not None,
                          n_carried=len(aliases), tiles_per_batch=tiles_per_batch),
        grid=(nt,),
        in_specs=in_specs,
        out_specs=out_specs,
        out_shape=out_shape,
        input_output_aliases=aliases,
        compiler_params=_cparams(1),
        name="norm_proj",
    )(*args)


_SEG = {}
_c = 0
for _name, _w in (("q_sb", D_SB), ("k_sb", D_SB), ("v_sb", D_SB), ("z_sb", D_SB),
                  ("q_mb", D_MB), ("k_mb", D_MB), ("v_mb", D_MB), ("z_mb", D_MB),
                  ("q_mem", D_MEM), ("z_mem", D_MEM)):
    _SEG[_name] = (_c, _c + _w)
    _c += _w
_T_NAMES = ("k_sb", "v_sb", "k_mb", "v_mb", "q_mb")


def _kv_weight_t(w_in_l):
    return jnp.concatenate([w_in_l[:, _SEG[s][0]:_SEG[s][1]] for s in _T_NAMES], axis=1).T.astype(BF16)


def _prompt_proj_plan(depth):
    t_rows = lambda i: (True, i * D_SB, (i + 1) * D_SB)
    outs = [("t_layers", D_SB, F32, depth)] * 4
    outs += [("t_tile", D_SB, BF16)] * 3 + [("t_tile", D_MB, F32)]
    outs += [("rows", D_MB, BF16), ("rows", D_SB, BF16), ("rows", D_MEM, BF16),
             ("rows", D_MIX, F32), ("block_means", D_MB, F32)]
    plan = [(t_rows(0), ((0, 0, "f32"), (4, 0, "bf16"))),
            (t_rows(1), ((1, 0, "f32"), (5, 0, "bf16"))),
            (t_rows(2), ((2, 0, "f32"), (12, 0, "block_mean"))),
            (t_rows(3), ((3, 0, "f32"), (6, 0, "bf16"))),
            (t_rows(4), ((7, 0, "f32"),))]
    plan += [((False,) + _SEG["k_mb"], ((8, 0, "bf16"),)),
             ((False,) + _SEG["q_sb"], ((9, 0, "bf16_scaled"),)),
             ((False,) + _SEG["q_mem"], ((10, 0, "bf16_scaled"),)),
             ((False,) + _SEG["z_sb"], ((11, 0, "silu"),)),
             ((False,) + _SEG["z_mb"], ((11, D_SB, "silu"),)),
             ((False,) + _SEG["z_mem"], ((11, D_SB + D_MB, "silu"),))]
    return outs, tuple(plan)


def _sample_proj_plan():
    outs = [("t_batch", D_SB, F32)] * 4
    outs += [("rows", D_MB, F32), ("rows", D_MB, F32)]
    outs += [("rows", D_SB, BF16), ("rows", D_MB, F32), ("rows", D_MEM, BF16),
             ("rows", D_MIX, F32)]
    plan = [((True, i * D_SB, (i + 1) * D_SB), ((i, 0, "f32"),)) for i in range(4)]
    plan += [((False,) + _SEG["k_mb"], ((4, 0, "f32"),)),
             ((False,) + _SEG["v_mb"], ((5, 0, "f32"),)),
             ((False,) + _SEG["q_sb"], ((6, 0, "bf16_scaled"),)),
             ((False,) + _SEG["q_mb"], ((7, 0, "f32"),)),
             ((False,) + _SEG["q_mem"], ((8, 0, "bf16_scaled"),)),
             ((False,) + _SEG["z_sb"], ((9, 0, "silu"),)),
             ((False,) + _SEG["z_mb"], ((9, D_SB, "silu"),)),
             ((False,) + _SEG["z_mem"], ((9, D_SB + D_MB, "silu"),))]
    return outs, tuple(plan)


def _mem_kv_plan():
    outs = [("t_tile", D_MEM, F32), ("t_tile", D_MEM, F32), ("t_tile", D_MEM, BF16), ("t_tile", D_MEM, BF16)]
    plan = (((True, 0, D_MEM), ((0, 0, "f32"), (2, 0, "bf16"))),
            ((True, D_MEM, 2 * D_MEM), ((1, 0, "f32"), (3, 0, "bf16"))))
    return outs, plan


def _merge_body(msb_ref, mmb_ref, mmem_ref, x_ref, w_ref, g_ref, o_ref):
    y = (_dot(msb_ref[...], w_ref[0:D_SB, :])
         + _dot(mmb_ref[...], w_ref[D_SB:D_SB + D_MB, :])
         + _dot(mmem_ref[...], w_ref[D_SB + D_MB:D_MIX, :]))
    ms = jnp.mean(y * y, axis=-1, keepdims=True)
    o_ref[...] = x_ref[...] + y * lax.rsqrt(ms + RMS_EPS) * g_ref[...]


def _merge(msb, mmb, mmem, x2, w_out_bf16, g_post, tm):
    n, d = x2.shape
    row = lambda w: pl.BlockSpec((tm, w), lambda i: (i, 0))
    return pl.pallas_call(
        _merge_body,
        grid=(n // tm,),
        in_specs=[row(D_SB), row(D_MB), row(D_MEM), row(d),
                  pl.BlockSpec((D_MIX, d), lambda i: (0, 0)),
                  pl.BlockSpec((1, d), lambda i: (0, 0))],
        out_specs=row(d),
        out_shape=jax.ShapeDtypeStruct((n, d), F32),
        compiler_params=_cparams(1),
        name="merge",
    )(msb, mmb, mmem, x2, w_out_bf16, g_post.reshape(1, d))


def _sb_prompt_body(q_ref, kt_ref, vt_ref, u_ref, g_ref, o_ref):
    tq = TOKEN_TILE
    n_batch = q_ref.shape[0]
    qi = pl.program_id(1)
    row = lax.broadcasted_iota(jnp.int32, (tq, tq), 0)
    col = lax.broadcasted_iota(jnp.int32, (tq, tq), 1)
    heads = [_head_lane_mask((tq, LANES), hh) for hh in range(PAIR)]
    chains = [(b, hh) for b in range(n_batch) for hh in range(PAIR)]
    qms = []
    for b, hh in chains:
        q = q_ref[b]
        qms.append(jnp.where(heads[hh], q, jnp.zeros_like(q)))

    def blocks(js, state, masks=None):
        return _sb_blocks(qms * len(js), [kt_ref[b, j] for j in js for b, _ in chains],
                          [vt_ref[b, j] for j in js for b, _ in chains], u_ref[...], state,
                          list(range(len(chains))) * len(js), masks)

    def carry_max(state):
        return jnp.max(functools.reduce(jnp.maximum, [carry for carry, _ in state]))

    diag = [col < row] * len(chains)
    zero = [(jnp.zeros((tq, 1), F32), jnp.zeros((tq, LANES), F32))] * len(chains)
    state = lax.cond(qi > 0, lambda: blocks([qi, qi - 1], zero, diag + [None] * len(chains)),
                     lambda: blocks([qi], zero, diag))

    def cond(loop):
        j, cmax, _ = loop
        return jnp.logical_and(j >= 0, cmax > -SB_EXIT)

    def body(loop):
        j, _, state = loop
        state = blocks([j], state)
        return j - 1, carry_max(state), state

    state = lax.while_loop(cond, body, (qi - 2, carry_max(state), state))[2]
    for b in range(n_batch):
        out = jnp.where(heads[0], state[PAIR * b][1], state[PAIR * b + 1][1])
        o_ref[b] = (out * g_ref[b]).astype(BF16)


def _sb_prompt(qb, ktb, vtb, gates, u, batch, seq):
    tq = TOKEN_TILE
    nq = seq // tq
    tile = lambda: pl.BlockSpec((batch, tq, LANES), lambda p, i: (0, i, p))
    keys = pl.BlockSpec((batch, nq, LANES, tq), lambda p, i: (0, 0, p, 0))
    by_batch = lambda a: a.reshape((batch, a.shape[0] // batch) + a.shape[1:])
    return pl.pallas_call(
        _sb_prompt_body,
        grid=(D_SB // LANES, nq),
        in_specs=[tile(), keys, keys, pl.BlockSpec((tq, tq), lambda p, i: (0, 0)), tile()],
        out_specs=tile(),
        out_shape=jax.ShapeDtypeStruct((batch, seq, D_SB), BF16),
        compiler_params=_cparams(2),
        name="sb_prompt",
    )(by_batch(qb), by_batch(ktb), by_batch(vtb), u, by_batch(gates)).reshape(batch * seq, D_SB)


def _t5_bucket_np(dist):
    n = np.maximum(dist, 0)
    nf = np.maximum(n, 1).astype(np.float32)
    large = MAX_EXACT + (np.log(nf / np.float32(MAX_EXACT)) / np.float32(math.log(MAX_DISTANCE / MAX_EXACT))
                         * np.float32(NUM_BUCKETS - MAX_EXACT)).astype(np.int32)
    large = np.minimum(large, NUM_BUCKETS - 1)
    return np.where(n < MAX_EXACT, n, large).astype(np.int32)


def _bias_tiles_body(tab_ref, bucket_ref, o_ref):
    h = pl.program_id(0)
    for t in range(2):
        bucket = bucket_ref[t]
        acc = jnp.zeros(bucket.shape, F32)
        for b in range(NUM_BUCKETS):
            acc = jnp.where(bucket == b, tab_ref[b, h], acc)
        o_ref[0, t] = acc - tab_ref[NUM_BUCKETS - 1, h]


def _bias_tiles(rel_bias):
    assert MOBA_BLOCK >= MAX_DISTANCE
    i = np.arange(MOBA_BLOCK)[None, :]
    j = np.arange(MOBA_BLOCK)[:, None]
    buckets = np.stack([_t5_bucket_np(i - j), _t5_bucket_np(MOBA_BLOCK + i - j)])
    return pl.pallas_call(
        _bias_tiles_body,
        grid=(H_MB,),
        in_specs=[pl.BlockSpec(memory_space=pltpu.SMEM),
                  pl.BlockSpec((2, MOBA_BLOCK, MOBA_BLOCK), lambda h: (0, 0, 0))],
        out_specs=pl.BlockSpec((1, 2, MOBA_BLOCK, MOBA_BLOCK), lambda h: (h, 0, 0, 0)),
        out_shape=jax.ShapeDtypeStruct((H_MB, 2, MOBA_BLOCK, MOBA_BLOCK), F32),
        compiler_params=_cparams(1),
        name="bias_tiles",
    )(rel_bias, jnp.asarray(buckets))


def _select_columns(gates):
    picked = []
    for n, g in enumerate(gates):
        beaten_by = [jnp.where((other > g) if m > n else (other >= g), 1.0, 0.0)
                     for m, other in enumerate(gates) if m != n]
        picked.append(functools.reduce(jnp.add, beaten_by) < MOBA_TOPK)
    return picked


def _select_blocks_t(gate, n_past):
    row = lax.broadcasted_iota(jnp.int32, gate.shape, 0)
    row_f = row.astype(F32)
    g = jnp.where(row < n_past, gate, NEG_INF)
    sel = jnp.zeros(gate.shape, F32)
    for _ in range(MOBA_TOPK):
        m = jnp.max(g, axis=0, keepdims=True)
        first = jnp.min(jnp.where(g == m, row_f, float(gate.shape[0])), axis=0, keepdims=True)
        pick = row_f == first
        sel = jnp.where(pick, 1.0, sel)
        g = jnp.where(pick, -jnp.inf, g)
    return jnp.where(row < n_past, sel, 0.0)


def _moba_prompt_body(qt_ref, k_ref, vt_ref, km_ref, bias_ref, g_ref, o_ref, pen_ref):
    c0 = pl.program_id(1)
    tq = MOBA_BLOCK
    n_batch = qt_ref.shape[0]
    n_rows = pen_ref.shape[1]
    chan = lax.broadcasted_iota(jnp.int32, (LANES, tq), 0)
    key = lax.broadcasted_iota(jnp.int32, (tq, tq), 0)
    qry = lax.broadcasted_iota(jnp.int32, (tq, tq), 1)
    ones = jnp.ones((2 * SUBLANES, tq), BF16)
    prev = jnp.maximum(c0 - 1, 0)
    heads = [(chan >= HEAD_DIM) if hh else (chan < HEAD_DIM) for hh in range(PAIR)]
    chains = [(b, hh) for b in range(n_batch) for hh in range(PAIR)]

    def keys_of(b, n):
        return k_ref[b, pl.ds(pl.multiple_of(n * tq, tq), tq), :]

    def values_of(b, hh, n):
        return jnp.concatenate([vt_ref[b, n][hh * HEAD_DIM:(hh + 1) * HEAD_DIM, :], ones], axis=0)

    def update(st, s, pen, vals):
        m, l, acc = st
        m_new = jnp.maximum(m, jnp.max(s, axis=0, keepdims=True) + pen)
        alpha = jnp.exp(m - m_new)
        pv = _dot(vals, jnp.exp(s - (m_new - pen)).astype(BF16))
        return m_new, alpha * l + pv[HEAD_DIM:HEAD_DIM + 1, :], alpha * acc + pv[:HEAD_DIM, :]

    qts, gates = [], []
    for b, hh in chains:
        qt = qt_ref[b, 0]
        qtb = (qt * Q_SCALE).astype(BF16)
        qts.append(jnp.where(heads[hh], qtb, jnp.zeros_like(qtb)))
        gates.append(jnp.dot(km_ref[b].T[:n_rows], jnp.where(heads[hh], qt, 0.0),
                             precision=lax.Precision.HIGHEST, preferred_element_type=F32))
    s_own = [_dot(keys_of(b, c0), qts[c]) for c, (b, _) in enumerate(chains)]
    s_prev = [_dot(keys_of(b, prev), qts[c]) for c, (b, _) in enumerate(chains)]
    state = []
    for c, (b, hh) in enumerate(chains):
        pen_ref[c] = jnp.where(_select_blocks_t(gates[c], c0) > 0.5, 0.0, NEG_INF)
        pen = pen_ref[c, pl.ds(prev, 1), :]
        s_o = jnp.where(key <= qry, s_own[c] + bias_ref[hh, 0], NEG_INF)
        s_p = s_prev[c] + bias_ref[hh, 1]
        m = jnp.maximum(jnp.max(s_o, axis=0, keepdims=True), jnp.max(s_p, axis=0, keepdims=True) + pen)
        pv = (_dot(values_of(b, hh, c0), jnp.exp(s_o - m).astype(BF16))
              + _dot(values_of(b, hh, prev), jnp.exp(s_p - (m - pen)).astype(BF16)))
        state += [m, pv[HEAD_DIM:HEAD_DIM + 1, :], pv[:HEAD_DIM, :]]

    per_trip = MOBA_BLOCKS_PER_TRIP

    def run_blocks(st, first, count):
        ns = [first + k for k in range(count)]
        scores = [[_dot(keys_of(b, n), qts[c]) for c, (b, _) in enumerate(chains)] for n in ns]
        st = list(st)
        for n, sc in zip(ns, scores):
            for c, (b, hh) in enumerate(chains):
                st[3 * c:3 * c + 3] = update(st[3 * c:3 * c + 3], sc[c],
                                             pen_ref[c, pl.ds(n, 1), :], values_of(b, hh, n))
        return tuple(st)

    whole = prev // per_trip
    final = lax.fori_loop(0, whole, lambda i, st: run_blocks(st, per_trip * i, per_trip), tuple(state))
    rest = prev - whole * per_trip
    for count in range(1, per_trip):
        final = lax.cond(rest == count, functools.partial(run_blocks, first=whole * per_trip, count=count),
                         lambda st: st, final)
    for b in range(n_batch):
        c = PAIR * b
        out_t = jnp.concatenate([final[3 * c + 2] / final[3 * c + 1], final[3 * c + 5] / final[3 * c + 4]],
                                axis=0)
        o_ref[b] = (out_t.T * g_ref[b]).astype(BF16)


def _moba_prompt(qt, kb, vtb, kmean, bias, gates, batch, seq):
    tq = MOBA_BLOCK
    nq = seq // tq
    n_rows = -(-nq // SUBLANES) * SUBLANES
    g_off = D_SB // LANES
    by_batch = lambda a: a.reshape((batch, a.shape[0] // batch) + a.shape[1:])
    return pl.pallas_call(
        _moba_prompt_body,
        grid=(D_MB // LANES, nq),
        in_specs=[pl.BlockSpec((batch, 1, LANES, tq), lambda p, i: (0, i, p, 0)),
                  pl.BlockSpec((batch, seq, LANES), lambda p, i: (0, 0, p)),
                  pl.BlockSpec((batch, nq, LANES, tq), lambda p, i: (0, 0, p, 0)),
                  pl.BlockSpec((batch, LANES, LANES), lambda p, i: (0, p, 0)),
                  pl.BlockSpec((PAIR, 2, tq, tq), lambda p, i: (p, 0, 0, 0)),
                  pl.BlockSpec((batch, tq, LANES), lambda p, i: (0, i, p + g_off))],
        out_specs=pl.BlockSpec((batch, tq, LANES), lambda p, i: (0, i, p)),
        out_shape=jax.ShapeDtypeStruct((batch, seq, D_MB), BF16),
        scratch_shapes=[pltpu.VMEM((batch * PAIR, n_rows, tq), F32)],
        compiler_params=_cparams(2),
        name="moba_prompt",
    )(by_batch(qt), by_batch(kb), by_batch(vtb), kmean, bias, by_batch(gates)).reshape(batch * seq, D_MB)


def _mem_prompt_body(q_ref, mkt_ref, mvt_ref, g_ref, o_ref):
    tq = q_ref.shape[0]
    for pr in range(D_MEM // LANES):
        chans = slice(pr * LANES, (pr + 1) * LANES)
        q = q_ref[:, chans]
        mkt = mkt_ref[0, chans, :]
        mvt = mvt_ref[0, chans, :]
        out = jnp.zeros((tq, LANES), F32)
        for hh in range(PAIR):
            head = _head_lane_mask((tq, LANES), hh)
            s = _dot(jnp.where(head, q, jnp.zeros_like(q)), mkt)
            p = jnp.exp(s - jnp.max(s, axis=1, keepdims=True))
            o = _dot_nt(p.astype(BF16), mvt) / jnp.sum(p, axis=1, keepdims=True)
            out = jnp.where(head, o, out)
        o_ref[:, chans] = (out * g_ref[:, chans]).astype(BF16)


def _mem_prompt(qb, mktb, mvtb, gates, batch, seq, tq):
    nq = seq // tq
    n_mem = mktb.shape[2]
    g_off = (D_SB + D_MB) // D_MEM
    mem = pl.BlockSpec((1, D_MEM, n_mem), lambda b, i: (b, 0, 0))
    return pl.pallas_call(
        _mem_prompt_body,
        grid=(batch, nq),
        in_specs=[pl.BlockSpec((tq, D_MEM), lambda b, i: (b * nq + i, 0)), mem, mem,
                  pl.BlockSpec((tq, D_MEM), lambda b, i: (b * nq + i, g_off))],
        out_specs=pl.BlockSpec((tq, D_MEM), lambda b, i: (b * nq + i, 0)),
        out_shape=jax.ShapeDtypeStruct((batch * seq, D_MEM), BF16),
        compiler_params=_cparams(2),
        name="mem_prompt",
    )(qb, mktb, mvtb, gates)


def _head_rows(x_row, width):
    r = lax.broadcasted_iota(jnp.int32, (SUBLANES, width), 0)
    lane = lax.broadcasted_iota(jnp.int32, (SUBLANES, width), 1)
    keep = (lane >= r * HEAD_DIM) & (lane < (r + 1) * HEAD_DIM)
    return jnp.where(keep, jnp.broadcast_to(x_row.astype(F32), (SUBLANES, width)), 0.0), keep


def _collapse_heads(x, keep):
    return jnp.sum(jnp.where(keep, x, 0.0), axis=0, keepdims=True)


def _sb_sample_chain(qm, u, k_refs, v_refs, carry, acc):
    n = len(k_refs)
    return _sb_blocks([qm] * n, [k_ref[0, 0].astype(BF16) for k_ref in reversed(k_refs)],
                      [v_ref[0, 0].astype(BF16) for v_ref in reversed(v_refs)], u, [(carry, acc)], [0] * n)[0]


def _sb_sample_head_body(pt_ref, q_ref, g_ref, u_ref, *refs, n_pages, group):
    del pt_ref
    n = n_pages * group
    k_refs, v_refs = refs[:n], refs[n:2 * n]
    o_ref, acc_ref, carry_ref, more_ref = refs[2 * n:]
    rows = [_head_rows(q_ref[g], D_SB) for g in range(group)]
    order = [(g, g * n_pages + p) for p in reversed(range(n_pages)) for g in range(group)]
    state = _sb_blocks([rows[g][0].astype(BF16) for g, _ in order],
                       [k_refs[i][0, 0].astype(BF16) for _, i in order],
                       [v_refs[i][0, 0].astype(BF16) for _, i in order], u_ref[...],
                       [(jnp.zeros((SUBLANES, 1), F32), jnp.zeros((SUBLANES, D_SB), F32))] * group,
                       [g for g, _ in order])
    live = lax.broadcasted_iota(jnp.int32, (SUBLANES, 1), 0) < H_SB
    for g, (carry, acc) in enumerate(state):
        o_ref[g] = (_collapse_heads(acc, rows[g][1]) * g_ref[g]).astype(BF16)
        acc_ref[g] = acc
        carry_ref[g] = jnp.broadcast_to(carry, (SUBLANES, LANES))
        more = jnp.max(jnp.where(live, carry, -jnp.inf)) > -SB_EXIT
        more_ref[g] = jnp.broadcast_to(more.astype(jnp.int32), (SUBLANES, LANES))


def _sb_sample_tail_body(pt_ref, more_ref, q_ref, g_ref, u_ref, acc_ref, carry_ref, *refs, n_pages):
    del pt_ref
    b = pl.program_id(0)
    k_refs, v_refs, o_ref = refs[:n_pages], refs[n_pages:2 * n_pages], refs[2 * n_pages]
    qm, keep = _head_rows(q_ref[0], D_SB)
    acc0 = acc_ref[0]

    def rest():
        return _sb_sample_chain(qm.astype(BF16), u_ref[...], k_refs, v_refs, carry_ref[0][:, 0:1], acc0)[1]

    acc = lax.cond(more_ref[b] != 0, rest, lambda: acc0)
    o_ref[0] = (_collapse_heads(acc, keep) * g_ref[0]).astype(BF16)


def _page_specs(layer, pages, width, gated=False):
    def spec(p):
        if gated:
            return pl.BlockSpec((1, 1, width, PAGE_SIZE),
                                lambda b, pt, more: (layer, jnp.where(more[b] != 0, pt[b, p], 0), 0, 0))
        return pl.BlockSpec((1, 1, width, PAGE_SIZE), lambda b, pt: (layer, pt[b, p], 0, 0))
    return [spec(p) for p in pages]


def _row_spec(width, off=0):
    return pl.BlockSpec((1, 1, width), lambda b, *_: (b, 0, off))


def _sb_sample(layer, page_table, qb, gates3, u, cache_kt, cache_vt):
    db, n_pages = page_table.shape
    group = SAMPLE_GROUP
    q3 = qb.reshape(db, 1, D_SB)
    head_pages = range(n_pages - SB_SAMPLE_HEAD_PAGES, n_pages)
    tail_pages = range(0, n_pages - SB_SAMPLE_HEAD_PAGES)
    u_spec = pl.BlockSpec((PAGE_SIZE, PAGE_SIZE), lambda b, *_: (0, 0))
    rows = lambda w, n=1: pl.BlockSpec((group, n, w), lambda b, *_: (b, 0, 0))
    group_pages = [pl.BlockSpec((1, 1, D_SB, PAGE_SIZE),
                                functools.partial(lambda b, pt, g, p: (layer, pt[b * group + g, p], 0, 0), g=g, p=p))
                   for g in range(group) for p in head_pages]
    n_head = len(group_pages)
    out, acc, carry, more = pl.pallas_call(
        functools.partial(_sb_sample_head_body, n_pages=len(head_pages), group=group),
        grid_spec=pltpu.PrefetchScalarGridSpec(
            num_scalar_prefetch=1,
            grid=(db // group,),
            in_specs=[rows(D_SB), rows(D_SB), u_spec] + group_pages + group_pages,
            out_specs=[rows(D_SB), rows(D_SB, SUBLANES), rows(LANES, SUBLANES), rows(LANES, SUBLANES)],
        ),
        out_shape=[jax.ShapeDtypeStruct((db, 1, D_SB), BF16),
                   jax.ShapeDtypeStruct((db, SUBLANES, D_SB), F32),
                   jax.ShapeDtypeStruct((db, SUBLANES, LANES), F32),
                   jax.ShapeDtypeStruct((db, SUBLANES, LANES), jnp.int32)],
        compiler_params=_cparams(1),
        name="sb_sample_head",
    )(page_table, q3, gates3, u, *([cache_kt] * n_head), *([cache_vt] * n_head))
    more = more[:, 0, 0]

    def tail():
        state = lambda w: pl.BlockSpec((1, SUBLANES, w), lambda b, *_: (b, 0, 0))
        return pl.pallas_call(
            functools.partial(_sb_sample_tail_body, n_pages=len(tail_pages)),
            grid_spec=pltpu.PrefetchScalarGridSpec(
                num_scalar_prefetch=2,
                grid=(db,),
                in_specs=[_row_spec(D_SB), _row_spec(D_SB), u_spec, state(D_SB), state(LANES)]
                         + _page_specs(layer, tail_pages, D_SB, gated=True)
                         + _page_specs(layer, tail_pages, D_SB, gated=True),
                out_specs=_row_spec(D_SB),
            ),
            out_shape=jax.ShapeDtypeStruct((db, 1, D_SB), BF16),
            compiler_params=_cparams(1),
            name="sb_sample_tail",
        )(page_table, more, q3, gates3, u, acc, carry,
          *([cache_kt] * len(tail_pages)), *([cache_vt] * len(tail_pages)))

    return lax.cond(jnp.any(more != 0), tail, lambda: out)


def _moba_sample_body(pt_ref, qf_ref, kn_ref, vn_ref, bias_ref, bown_ref, g_ref, *refs, n_pages):
    del pt_ref
    k_refs, v_refs, o_ref = refs[:n_pages], refs[n_pages:2 * n_pages], refs[2 * n_pages]
    pages_per_block = MOBA_BLOCK // PAGE_SIZE
    n_past = n_pages // pages_per_block
    qf, keep = _head_rows(qf_ref[0], D_MB)
    qs = qf * Q_SCALE
    q_hi = qs.astype(BF16).astype(F32)
    q_parts = jnp.concatenate([q_hi, qs - q_hi], axis=0).astype(BF16)

    scores = []
    for p in range(n_pages):
        both = _dot(q_parts, k_refs[p][0, 0].astype(BF16))
        scores.append(both[:SUBLANES] + both[SUBLANES:])
    gates = [jnp.sum(functools.reduce(jnp.add, scores[n * pages_per_block:(n + 1) * pages_per_block]),
                     axis=1, keepdims=True) * (1.0 / (Q_SCALE * MOBA_BLOCK)) for n in range(n_past)]
    sel = _select_columns(gates)

    kn = kn_ref[0].astype(BF16).astype(F32)
    s_own = jnp.sum(qf * Q_SCALE * kn, axis=1, keepdims=True) + bown_ref[:, 0:1]
    masked = [jnp.where(sel[p // pages_per_block], scores[p] + bias_ref[:, p * PAGE_SIZE:(p + 1) * PAGE_SIZE],
                        NEG_INF) for p in range(n_pages)]
    m = jnp.maximum(s_own, jnp.max(functools.reduce(jnp.maximum, masked), axis=1, keepdims=True))
    p_own = jnp.exp(s_own - m)
    weights = [jnp.exp(s - m) for s in masked]
    l = p_own + jnp.sum(functools.reduce(jnp.add, weights), axis=1, keepdims=True)
    acc = p_own * vn_ref[0]
    for w, v_ref in zip(weights, v_refs):
        acc = acc + _dot_nt(w.astype(BF16), v_ref[0, 0].astype(BF16))
    o_ref[0] = (_collapse_heads(acc / l, keep) * g_ref[0]).astype(BF16)


def _moba_sample(layer, page_table, qf, k_new, v_new, bias_keys, bias_own, gates3, cache_kt, cache_vt):
    db, n_pages = page_table.shape
    past = n_pages * PAGE_SIZE
    grid_spec = pltpu.PrefetchScalarGridSpec(
        num_scalar_prefetch=1,
        grid=(db,),
        in_specs=[_row_spec(D_MB), _row_spec(D_MB), _row_spec(D_MB),
                  pl.BlockSpec((SUBLANES, past), lambda b, pt: (0, 0)),
                  pl.BlockSpec((SUBLANES, LANES), lambda b, pt: (0, 0)),
                  _row_spec(D_MB, D_SB // D_MB)]
                 + _page_specs(layer, range(n_pages), D_MB) + _page_specs(layer, range(n_pages), D_MB),
        out_specs=_row_spec(D_MB),
    )
    r3 = lambda a: a.reshape(db, 1, D_MB)
    return pl.pallas_call(
        functools.partial(_moba_sample_body, n_pages=n_pages),
        grid_spec=grid_spec,
        out_shape=jax.ShapeDtypeStruct((db, 1, D_MB), BF16),
        compiler_params=_cparams(1),
        name="moba_sample",
    )(page_table, r3(qf), r3(k_new), r3(v_new), bias_keys, bias_own, gates3,
      *([cache_kt] * n_pages), *([cache_vt] * n_pages))


def _mem_sample_body(q_ref, mkt_ref, mvt_ref, g_ref, o_ref, *, group):
    rows = [_head_rows(q_ref[g], D_MEM) for g in range(group)]
    scores = [_dot(rows[g][0].astype(BF16), mkt_ref[0, g].astype(BF16)) for g in range(group)]
    probs = [jnp.exp(s - jnp.max(s, axis=1, keepdims=True)) for s in scores]
    outs = [_dot_nt(p.astype(BF16), mvt_ref[0, g].astype(BF16)) for g, p in enumerate(probs)]
    for g in range(group):
        o = outs[g] / jnp.sum(probs[g], axis=1, keepdims=True)
        o_ref[g] = (_collapse_heads(o, rows[g][1]) * g_ref[g]).astype(BF16)


def _mem_sample(layer, qb, gates3, mem_kt, mem_vt):
    db, n_mem = mem_kt.shape[1], mem_kt.shape[3]
    group = MEM_SAMPLE_GROUP
    mem = pl.BlockSpec((1, group, D_MEM, n_mem), lambda b: (layer, b, 0, 0))
    row = lambda off: pl.BlockSpec((group, 1, D_MEM), lambda b: (b, 0, off))
    return pl.pallas_call(
        functools.partial(_mem_sample_body, group=group),
        grid=(db // group,),
        in_specs=[row(0), mem, mem, row((D_SB + D_MB) // D_MEM)],
        out_specs=row(0),
        out_shape=jax.ShapeDtypeStruct((db, 1, D_MEM), BF16),
        compiler_params=_cparams(1),
        name="mem_sample",
    )(qb.reshape(db, 1, D_MEM), mem_kt, mem_vt, gates3)


def _upper_ones(n):
    i = np.arange(n)
    return jnp.asarray((i[:, None] >= i[None, :]).astype(np.float32), dtype=BF16)


def _channel_major(cache, heads):
    lead = cache.shape[:-3]
    t = jnp.moveaxis(cache, -3, -1)
    return t.reshape(lead + (heads * HEAD_DIM, cache.shape[-3]))


def _token_major(x, heads):
    lead = x.shape[:-2]
    return jnp.moveaxis(x.reshape(lead + (heads, HEAD_DIM, x.shape[-1])), -1, -3)


def kernel(x_prompt, x_sample, cache_sb_k, cache_sb_v, cache_moba_k, cache_moba_v, cache_mem_k,
           cache_mem_v, page_table, mem_prompt, rel_bias, g_pre, g_post, g_mem, w_in, w_out, w_mem_kv):
    batch, seq, d_model = x_prompt.shape
    db, dec_seq, _ = x_sample.shape
    depth = w_in.shape[0]
    n_mem = mem_prompt.shape[1]
    n_pages = page_table.shape[1]
    past = n_pages * PAGE_SIZE
    assert dec_seq == 1 and seq % TOKEN_TILE == 0 and past % MOBA_BLOCK == 0
    assert past >= 2 * MOBA_BLOCK and db % SAMPLE_GROUP == 0 and db % MEM_SAMPLE_GROUP == 0

    tm = TOKEN_TILE
    tiles = seq // tm
    u_block = _upper_ones(TOKEN_TILE)
    u_page = _upper_ones(PAGE_SIZE)
    p_outs, p_plan = _prompt_proj_plan(depth)
    s_outs, s_plan = _sample_proj_plan()
    kv_outs, kv_plan = _mem_kv_plan()

    bias = _bias_tiles(rel_bias)
    pad = jnp.zeros((SUBLANES - H_MB, past), F32)
    bias_keys = jnp.concatenate(
        [jnp.concatenate([jnp.zeros((H_MB, past - MOBA_BLOCK), F32), bias[:, 1, :, 0]], axis=1), pad], axis=0)
    bias_own = jnp.concatenate(
        [jnp.broadcast_to(bias[:, 0, 0, 0:1], (H_MB, LANES)), pad[:, :LANES]], axis=0)

    sb_kt, sb_vt = _channel_major(cache_sb_k, H_SB), _channel_major(cache_sb_v, H_SB)
    mb_kt, mb_vt = _channel_major(cache_moba_k, H_MB), _channel_major(cache_moba_v, H_MB)
    mem_kt, mem_vt = _channel_major(cache_mem_k, H_MEM), _channel_major(cache_mem_v, H_MEM)

    xp = x_prompt.reshape(batch * seq, d_model)
    xs = x_sample.reshape(db, d_model)
    mem2 = mem_prompt.reshape(batch * n_mem, d_model)
    pt = page_table.astype(jnp.int32)
    outs = [[] for _ in range(10)]
    kv_layers = None
    for l in range(depth):
        w_in_l = w_in[l].astype(BF16)
        w_kv_t = _kv_weight_t(w_in[l])
        w_out_l = w_out[l].astype(BF16)

        *kv_layers, ktb_sb, vtb_sb, vtb_mb, qt_mb, kb_mb, qb_sb, qb_mem, gates, kmean = _norm_proj(
            xp, g_pre[l], w_in_l, w_kv_t, p_outs, p_plan, tm, tiles, layer=l, carried=kv_layers)
        mkt, mvt, mktb, mvtb = _norm_proj(mem2, g_mem[l], None, w_mem_kv[l].T.astype(BF16),
                                          kv_outs, kv_plan, n_mem, 1)
        m_sb = _sb_prompt(qb_sb, ktb_sb, vtb_sb, gates, u_block, batch, seq)
        m_mb = _moba_prompt(qt_mb, kb_mb, vtb_mb, kmean, bias, gates, batch, seq)
        m_mem = _mem_prompt(qb_mem, mktb, mvtb, gates, batch, seq, 2 * tm)
        xp = _merge(m_sb, m_mb, m_mem, xp, w_out_l, g_post[l], 2 * tm)
        outs[4].append(_token_major(mkt, H_MEM))
        outs[5].append(_token_major(mvt, H_MEM))

        (kt_sb, vt_sb, kt_mb, vt_mb, k_mb, v_mb, qb_sb, qf_mb, qb_mem, gates) = _norm_proj(
            xs, g_pre[l], w_in_l, w_kv_t, s_outs, s_plan, db, 1)
        gates3 = gates.reshape(db, 1, D_MIX)
        s_sb = _sb_sample(l, pt, qb_sb, gates3, u_page, sb_kt, sb_vt)
        s_mb = _moba_sample(l, pt, qf_mb, k_mb, v_mb, bias_keys, bias_own, gates3, mb_kt, mb_vt)
        s_mem = _mem_sample(l, qb_mem, gates3, mem_kt, mem_vt)
        xs = _merge(s_sb.reshape(db, D_SB), s_mb.reshape(db, D_MB), s_mem.reshape(db, D_MEM),
                    xs, w_out_l, g_post[l], db)
        for i, (a, h) in enumerate(((kt_sb, H_SB), (vt_sb, H_SB), (kt_mb, H_MB), (vt_mb, H_MB))):
            outs[6 + i].append(_token_major(a, h).reshape(db, 1, h, HEAD_DIM))

    prompt_kv = [_token_major(a, h) for a, h in zip(kv_layers, (H_SB, H_SB, H_MB, H_MB))]
    return ((xp.reshape(batch, seq, d_model), xs.reshape(db, 1, d_model)) + tuple(prompt_kv)
            + tuple(jnp.stack(o) for o in outs[4:]))
```

```python
import functools
import math

import numpy as np
import jax
import jax.numpy as jnp
from jax import lax
from jax.experimental import pallas as pl
from jax.experimental.pallas import tpu as pltpu

HEAD_DIM = 64
H_SB, H_MB, H_MEM = 6, 6, 4
D_SB, D_MB, D_MEM = H_SB * HEAD_DIM, H_MB * HEAD_DIM, H_MEM * HEAD_DIM
D_MIX = D_SB + D_MB + D_MEM
MOBA_BLOCK = 256
MOBA_TOPK = 3
PAGE_SIZE = 128
NUM_BUCKETS = 32
MAX_EXACT = 16
MAX_DISTANCE = 128
RMS_EPS = 1e-6
NEG_INF = -1e30
Q_SCALE = 1.0 / math.sqrt(HEAD_DIM)

LANES = 128
SUBLANES = 8
PAIR = LANES // HEAD_DIM
VMEM_LIMIT = 48 * 1024 * 1024
TOKEN_TILE = MOBA_BLOCK

SB_EXIT = 110.0
SB_SAMPLE_HEAD_PAGES = 2
MOBA_BLOCKS_PER_TRIP = 3
SAMPLE_GROUP = 8
MEM_SAMPLE_GROUP = 8
MOBA_SAMPLE_GROUP = 2

BF16 = jnp.bfloat16
F32 = jnp.float32
_NT = (((1,), (1,)), ((), ()))


def _cparams(n_axes):
    return pltpu.CompilerParams(dimension_semantics=("arbitrary",) * n_axes,
                                vmem_limit_bytes=VMEM_LIMIT)


def _dot(a, b):
    return jnp.dot(a, b, preferred_element_type=F32)


def _dot_nt(a, b):
    return lax.dot_general(a, b, _NT, preferred_element_type=F32)


def _split_bf16(x):
    hi = x.astype(BF16)
    return hi, (x - hi.astype(F32)).astype(BF16)


def _split_dot(x, u):
    hi, lo = _split_bf16(x)
    return _dot(hi, u) + _dot(lo, u)


def _head_lane_mask(shape, hh):
    lane = lax.broadcasted_iota(jnp.int32, shape, len(shape) - 1)
    return (lane >= HEAD_DIM) if hh else (lane < HEAD_DIM)


def _log_keep(z):
    return jnp.minimum(-z, 0.0) - jnp.log(1.0 + jnp.exp(-jnp.abs(z)))


def _sb_blocks(qms, kts, vts, u, state, chain_of, masks=None):
    masks = masks or [None] * len(qms)
    zs = [_dot(qm, kt) for qm, kt in zip(qms, kts)]
    log_keeps = [_log_keep(z) if mask is None else jnp.where(mask, _log_keep(z), 0.0)
                 for z, mask in zip(zs, masks)]
    sums = [_split_dot(log_keep, u) for log_keep in log_keeps]
    state = list(state)
    for z, total, vt, c, mask in zip(zs, sums, vts, chain_of, masks):
        carry, acc = state[c]
        a = jnp.exp(z + total + carry)
        if mask is not None:
            a = jnp.where(mask, a, 0.0)
        state[c] = (carry + total[:, 0:1], acc + _dot_nt(a.astype(BF16), vt))
    return state


def _norm_proj_body(*refs, plan, has_w, has_wt, n_carried, tiles_per_batch):
    x_ref, g_ref = refs[:2]
    rest = list(refs[2:])
    w_ref = rest.pop(0) if has_w else None
    wt_ref = rest.pop(0) if has_wt else None
    out_refs = rest[n_carried:]
    x = x_ref[...]
    ms = jnp.mean(x * x, axis=-1, keepdims=True)
    h = (x * lax.rsqrt(ms + RMS_EPS) * g_ref[...]).astype(BF16)
    for (transposed, lo, hi), writes in plan:
        p = _dot_nt(wt_ref[lo:hi, :], h) if transposed else _dot(h, w_ref[:, lo:hi])
        for oi, dlo, kind in writes:
            if kind == "block_mean":
                ref = out_refs[oi]
                t = pl.program_id(0) % tiles_per_batch
                lane = lax.broadcasted_iota(jnp.int32, ref.shape[1:], 1)
                mean = jnp.sum(p, axis=1, keepdims=True) * (1.0 / p.shape[1])

                @pl.when(t == 0)
                def _(ref=ref):
                    ref[0] = jnp.zeros(ref.shape[1:], F32)

                ref[0] = jnp.where(lane == t, mean, ref[0])
                continue
            if kind == "f32":
                val = p
            elif kind == "bf16":
                val = p.astype(BF16)
            elif kind == "bf16_scaled":
                val = (p * Q_SCALE).astype(BF16)
            else:
                val = p * (1.0 / (1.0 + jnp.exp(-p)))
            if transposed:
                ref = out_refs[oi]
                ref[(0,) * (len(ref.shape) - 2)] = val
            else:
                out_refs[oi][:, dlo:dlo + (hi - lo)] = val


def _norm_proj(x2, gain, w, wt, outs, plan, tm, tiles_per_batch, layer=0, carried=None):
    n, d = x2.shape
    nt = n // tm
    in_specs = [pl.BlockSpec((tm, d), lambda i: (i, 0)), pl.BlockSpec((1, d), lambda i: (0, 0))]
    args = [x2, gain.reshape(1, d)]
    for m in (w, wt):
        if m is not None:
            in_specs.append(pl.BlockSpec(m.shape, lambda i: (0, 0)))
            args.append(m)
    carried = list(carried or [])
    aliases = {}
    out_specs, out_shape = [], []
    for kind, width, dt, *extra in outs:
        if kind == "rows":
            out_specs.append(pl.BlockSpec((tm, width), lambda i: (i, 0)))
            out_shape.append(jax.ShapeDtypeStruct((n, width), dt))
        elif kind == "t_batch":
            out_specs.append(pl.BlockSpec((1, width, tm),
                                          lambda i: (i // tiles_per_batch, 0, i % tiles_per_batch)))
            out_shape.append(jax.ShapeDtypeStruct((nt // tiles_per_batch, width, tiles_per_batch * tm), dt))
        elif kind == "block_means":
            assert tiles_per_batch <= LANES
            out_specs.append(pl.BlockSpec((1, width, LANES), lambda i: (i // tiles_per_batch, 0, 0)))
            out_shape.append(jax.ShapeDtypeStruct((nt // tiles_per_batch, width, LANES), dt))
        elif kind == "t_layers":
            out_specs.append(pl.BlockSpec((1, 1, width, tm),
                                          lambda i: (layer, i // tiles_per_batch, 0, i % tiles_per_batch)))
            out_shape.append(jax.ShapeDtypeStruct(
                (extra[0], nt // tiles_per_batch, width, tiles_per_batch * tm), dt))
            if carried:
                aliases[len(args)] = len(out_shape) - 1
                in_specs.append(pl.BlockSpec(memory_space=pl.ANY))
                args.append(carried.pop(0))
        else:
            out_specs.append(pl.BlockSpec((1, width, tm), lambda i: (i, 0, 0)))
            out_shape.append(jax.ShapeDtypeStruct((nt, width, tm), dt))
    return pl.pallas_call(
        functools.partial(_norm_proj_body, plan=plan, has_w=w is not None, has_wt=wt is not None,
                          n_carried=len(aliases), tiles_per_batch=tiles_per_batch),
        grid=(nt,),
        in_specs=in_specs,
        out_specs=out_specs,
        out_shape=out_shape,
        input_output_aliases=aliases,
        compiler_params=_cparams(1),
        name="norm_proj",
    )(*args)


_SEG = {}
_c = 0
for _name, _w in (("q_sb", D_SB), ("k_sb", D_SB), ("v_sb", D_SB), ("z_sb", D_SB),
                  ("q_mb", D_MB), ("k_mb", D_MB), ("v_mb", D_MB), ("z_mb", D_MB),
                  ("q_mem", D_MEM), ("z_mem", D_MEM)):
    _SEG[_name] = (_c, _c + _w)
    _c += _w
_T_NAMES = ("k_sb", "v_sb", "k_mb", "v_mb", "q_mb")


def _kv_weight_t(w_in_l):
    return jnp.concatenate([w_in_l[:, _SEG[s][0]:_SEG[s][1]] for s in _T_NAMES], axis=1).T.astype(BF16)


def _prompt_proj_plan(depth):
    t_rows = lambda i: (True, i * D_SB, (i + 1) * D_SB)
    outs = [("t_layers", D_SB, F32, depth)] * 4
    outs += [("t_tile", D_SB, BF16)] * 3 + [("t_tile", D_MB, F32)]
    outs += [("rows", D_MB, BF16), ("rows", D_SB, BF16), ("rows", D_MEM, BF16),
             ("rows", D_MIX, F32), ("block_means", D_MB, F32)]
    plan = [(t_rows(0), ((0, 0, "f32"), (4, 0, "bf16"))),
            (t_rows(1), ((1, 0, "f32"), (5, 0, "bf16"))),
            (t_rows(2), ((2, 0, "f32"), (12, 0, "block_mean"))),
            (t_rows(3), ((3, 0, "f32"), (6, 0, "bf16"))),
            (t_rows(4), ((7, 0, "f32"),))]
    plan += [((False,) + _SEG["k_mb"], ((8, 0, "bf16"),)),
             ((False,) + _SEG["q_sb"], ((9, 0, "bf16_scaled"),)),
             ((False,) + _SEG["q_mem"], ((10, 0, "bf16_scaled"),)),
             ((False,) + _SEG["z_sb"], ((11, 0, "silu"),)),
             ((False,) + _SEG["z_mb"], ((11, D_SB, "silu"),)),
             ((False,) + _SEG["z_mem"], ((11, D_SB + D_MB, "silu"),))]
    return outs, tuple(plan)


def _sample_proj_plan():
    outs = [("t_batch", D_SB, F32)] * 4
    outs += [("rows", D_MB, F32), ("rows", D_MB, F32)]
    outs += [("rows", D_SB, BF16), ("rows", D_MB, F32), ("rows", D_MEM, BF16),
             ("rows", D_MIX, F32)]
    plan = [((True, i * D_SB, (i + 1) * D_SB), ((i, 0, "f32"),)) for i in range(4)]
    plan += [((False,) + _SEG["k_mb"], ((4, 0, "f32"),)),
             ((False,) + _SEG["v_mb"], ((5, 0, "f32"),)),
             ((False,) + _SEG["q_sb"], ((6, 0, "bf16_scaled"),)),
             ((False,) + _SEG["q_mb"], ((7, 0, "f32"),)),
             ((False,) + _SEG["q_mem"], ((8, 0, "bf16_scaled"),)),
             ((False,) + _SEG["z_sb"], ((9, 0, "silu"),)),
             ((False,) + _SEG["z_mb"], ((9, D_SB, "silu"),)),
             ((False,) + _SEG["z_mem"], ((9, D_SB + D_MB, "silu"),))]
    return outs, tuple(plan)


def _mem_kv_plan():
    outs = [("t_tile", D_MEM, F32), ("t_tile", D_MEM, F32), ("t_tile", D_MEM, BF16), ("t_tile", D_MEM, BF16)]
    plan = (((True, 0, D_MEM), ((0, 0, "f32"), (2, 0, "bf16"))),
            ((True, D_MEM, 2 * D_MEM), ((1, 0, "f32"), (3, 0, "bf16"))))
    return outs, plan


def _merge_body(msb_ref, mmb_ref, mmem_ref, x_ref, w_ref, g_ref, o_ref):
    y = (_dot(msb_ref[...], w_ref[0:D_SB, :])
         + _dot(mmb_ref[...], w_ref[D_SB:D_SB + D_MB, :])
         + _dot(mmem_ref[...], w_ref[D_SB + D_MB:D_MIX, :]))
    ms = jnp.mean(y * y, axis=-1, keepdims=True)
    o_ref[...] = x_ref[...] + y * lax.rsqrt(ms + RMS_EPS) * g_ref[...]


def _merge(msb, mmb, mmem, x2, w_out_bf16, g_post, tm):
    n, d = x2.shape
    row = lambda w: pl.BlockSpec((tm, w), lambda i: (i, 0))
    return pl.pallas_call(
        _merge_body,
        grid=(n // tm,),
        in_specs=[row(D_SB), row(D_MB), row(D_MEM), row(d),
                  pl.BlockSpec((D_MIX, d), lambda i: (0, 0)),
                  pl.BlockSpec((1, d), lambda i: (0, 0))],
        out_specs=row(d),
        out_shape=jax.ShapeDtypeStruct((n, d), F32),
        compiler_params=_cparams(1),
        name="merge",
    )(msb, mmb, mmem, x2, w_out_bf16, g_post.reshape(1, d))


def _sb_prompt_body(q_ref, kt_ref, vt_ref, u_ref, g_ref, o_ref):
    tq = TOKEN_TILE
    n_batch = q_ref.shape[0]
    qi = pl.program_id(1)
    row = lax.broadcasted_iota(jnp.int32, (tq, tq), 0)
    col = lax.broadcasted_iota(jnp.int32, (tq, tq), 1)
    heads = [_head_lane_mask((tq, LANES), hh) for hh in range(PAIR)]
    chains = [(b, hh) for b in range(n_batch) for hh in range(PAIR)]
    qms = []
    for b, hh in chains:
        q = q_ref[b]
        qms.append(jnp.where(heads[hh], q, jnp.zeros_like(q)))

    def blocks(js, state, masks=None):
        return _sb_blocks(qms * len(js), [kt_ref[b, j] for j in js for b, _ in chains],
                          [vt_ref[b, j] for j in js for b, _ in chains], u_ref[...], state,
                          list(range(len(chains))) * len(js), masks)

    def carry_max(state):
        return jnp.max(functools.reduce(jnp.maximum, [carry for carry, _ in state]))

    diag = [col < row] * len(chains)
    zero = [(jnp.zeros((tq, 1), F32), jnp.zeros((tq, LANES), F32))] * len(chains)
    state = lax.cond(qi > 0, lambda: blocks([qi, qi - 1], zero, diag + [None] * len(chains)),
                     lambda: blocks([qi], zero, diag))

    def cond(loop):
        j, cmax, _ = loop
        return jnp.logical_and(j >= 0, cmax > -SB_EXIT)

    def body(loop):
        j, _, state = loop
        state = blocks([j], state)
        return j - 1, carry_max(state), state

    state = lax.while_loop(cond, body, (qi - 2, carry_max(state), state))[2]
    for b in range(n_batch):
        out = jnp.where(heads[0], state[PAIR * b][1], state[PAIR * b + 1][1])
        o_ref[b] = (out * g_ref[b]).astype(BF16)


def _sb_prompt(qb, ktb, vtb, gates, u, batch, seq):
    tq = TOKEN_TILE
    nq = seq // tq
    tile = lambda: pl.BlockSpec((batch, tq, LANES), lambda p, i: (0, i, p))
    keys = pl.BlockSpec((batch, nq, LANES, tq), lambda p, i: (0, 0, p, 0))
    by_batch = lambda a: a.reshape((batch, a.shape[0] // batch) + a.shape[1:])
    return pl.pallas_call(
        _sb_prompt_body,
        grid=(D_SB // LANES, nq),
        in_specs=[tile(), keys, keys, pl.BlockSpec((tq, tq), lambda p, i: (0, 0)), tile()],
        out_specs=tile(),
        out_shape=jax.ShapeDtypeStruct((batch, seq, D_SB), BF16),
        compiler_params=_cparams(2),
        name="sb_prompt",
    )(by_batch(qb), by_batch(ktb), by_batch(vtb), u, by_batch(gates)).reshape(batch * seq, D_SB)


def _t5_bucket_np(dist):
    n = np.maximum(dist, 0)
    nf = np.maximum(n, 1).astype(np.float32)
    large = MAX_EXACT + (np.log(nf / np.float32(MAX_EXACT)) / np.float32(math.log(MAX_DISTANCE / MAX_EXACT))
                         * np.float32(NUM_BUCKETS - MAX_EXACT)).astype(np.int32)
    large = np.minimum(large, NUM_BUCKETS - 1)
    return np.where(n < MAX_EXACT, n, large).astype(np.int32)


def _bias_tiles_body(tab_ref, bucket_ref, o_ref):
    h = pl.program_id(0)
    for t in range(2):
        bucket = bucket_ref[t]
        acc = jnp.zeros(bucket.shape, F32)
        for b in range(NUM_BUCKETS):
            acc = jnp.where(bucket == b, tab_ref[b, h], acc)
        o_ref[0, t] = acc - tab_ref[NUM_BUCKETS - 1, h]


def _bias_tiles(rel_bias):
    assert MOBA_BLOCK >= MAX_DISTANCE
    i = np.arange(MOBA_BLOCK)[None, :]
    j = np.arange(MOBA_BLOCK)[:, None]
    buckets = np.stack([_t5_bucket_np(i - j), _t5_bucket_np(MOBA_BLOCK + i - j)])
    return pl.pallas_call(
        _bias_tiles_body,
        grid=(H_MB,),
        in_specs=[pl.BlockSpec(memory_space=pltpu.SMEM),
                  pl.BlockSpec((2, MOBA_BLOCK, MOBA_BLOCK), lambda h: (0, 0, 0))],
        out_specs=pl.BlockSpec((1, 2, MOBA_BLOCK, MOBA_BLOCK), lambda h: (h, 0, 0, 0)),
        out_shape=jax.ShapeDtypeStruct((H_MB, 2, MOBA_BLOCK, MOBA_BLOCK), F32),
        compiler_params=_cparams(1),
        name="bias_tiles",
    )(rel_bias, jnp.asarray(buckets))


def _select_columns(gates):
    picked = []
    for n, g in enumerate(gates):
        beaten_by = [jnp.where((other > g) if m > n else (other >= g), 1.0, 0.0)
                     for m, other in enumerate(gates) if m != n]
        picked.append(functools.reduce(jnp.add, beaten_by) < MOBA_TOPK)
    return picked


def _select_blocks_t(gate, n_past):
    row = lax.broadcasted_iota(jnp.int32, gate.shape, 0)
    row_f = row.astype(F32)
    g = jnp.where(row < n_past, gate, NEG_INF)
    sel = jnp.zeros(gate.shape, F32)
    for _ in range(MOBA_TOPK):
        m = jnp.max(g, axis=0, keepdims=True)
        first = jnp.min(jnp.where(g == m, row_f, float(gate.shape[0])), axis=0, keepdims=True)
        pick = row_f == first
        sel = jnp.where(pick, 1.0, sel)
        g = jnp.where(pick, -jnp.inf, g)
    return jnp.where(row < n_past, sel, 0.0)


def _moba_prompt_body(qt_ref, k_ref, vt_ref, km_ref, bias_ref, g_ref, o_ref, pen_ref):
    c0 = pl.program_id(1)
    tq = MOBA_BLOCK
    n_batch = qt_ref.shape[0]
    n_rows = pen_ref.shape[1]
    chan = lax.broadcasted_iota(jnp.int32, (LANES, tq), 0)
    key = lax.broadcasted_iota(jnp.int32, (tq, tq), 0)
    qry = lax.broadcasted_iota(jnp.int32, (tq, tq), 1)
    ones = jnp.ones((2 * SUBLANES, tq), BF16)
    prev = jnp.maximum(c0 - 1, 0)
    heads = [(chan >= HEAD_DIM) if hh else (chan < HEAD_DIM) for hh in range(PAIR)]
    chains = [(b, hh) for b in range(n_batch) for hh in range(PAIR)]

    def keys_of(b, n):
        return k_ref[b, pl.ds(pl.multiple_of(n * tq, tq), tq), :]

    def values_of(b, hh, n):
        return jnp.concatenate([vt_ref[b, n][hh * HEAD_DIM:(hh + 1) * HEAD_DIM, :], ones], axis=0)

    def update(st, s, pen, vals):
        m, l, acc = st
        m_new = jnp.maximum(m, jnp.max(s, axis=0, keepdims=True) + pen)
        alpha = jnp.exp(m - m_new)
        pv = _dot(vals, jnp.exp(s - (m_new - pen)).astype(BF16))
        return m_new, alpha * l + pv[HEAD_DIM:HEAD_DIM + 1, :], alpha * acc + pv[:HEAD_DIM, :]

    qts, gates = [], []
    for b, hh in chains:
        qt = qt_ref[b, 0]
        qtb = (qt * Q_SCALE).astype(BF16)
        qts.append(jnp.where(heads[hh], qtb, jnp.zeros_like(qtb)))
        gates.append(jnp.dot(km_ref[b].T[:n_rows], jnp.where(heads[hh], qt, 0.0),
                             precision=lax.Precision.HIGHEST, preferred_element_type=F32))
    s_own = [_dot(keys_of(b, c0), qts[c]) for c, (b, _) in enumerate(chains)]
    s_prev = [_dot(keys_of(b, prev), qts[c]) for c, (b, _) in enumerate(chains)]
    state = []
    for c, (b, hh) in enumerate(chains):
        pen_ref[c] = jnp.where(_select_blocks_t(gates[c], c0) > 0.5, 0.0, NEG_INF)
        pen = pen_ref[c, pl.ds(prev, 1), :]
        s_o = jnp.where(key <= qry, s_own[c] + bias_ref[hh, 0], NEG_INF)
        s_p = s_prev[c] + bias_ref[hh, 1]
        m = jnp.maximum(jnp.max(s_o, axis=0, keepdims=True), jnp.max(s_p, axis=0, keepdims=True) + pen)
        pv = (_dot(values_of(b, hh, c0), jnp.exp(s_o - m).astype(BF16))
              + _dot(values_of(b, hh, prev), jnp.exp(s_p - (m - pen)).astype(BF16)))
        state += [m, pv[HEAD_DIM:HEAD_DIM + 1, :], pv[:HEAD_DIM, :]]

    per_trip = MOBA_BLOCKS_PER_TRIP

    def run_blocks(st, first, count):
        ns = [first + k for k in range(count)]
        scores = [[_dot(keys_of(b, n), qts[c]) for c, (b, _) in enumerate(chains)] for n in ns]
        st = list(st)
        for n, sc in zip(ns, scores):
            for c, (b, hh) in enumerate(chains):
                st[3 * c:3 * c + 3] = update(st[3 * c:3 * c + 3], sc[c],
                                             pen_ref[c, pl.ds(n, 1), :], values_of(b, hh, n))
        return tuple(st)

    whole = prev // per_trip
    final = lax.fori_loop(0, whole, lambda i, st: run_blocks(st, per_trip * i, per_trip), tuple(state))
    rest = prev - whole * per_trip
    for count in range(1, per_trip):
        final = lax.cond(rest == count, functools.partial(run_blocks, first=whole * per_trip, count=count),
                         lambda st: st, final)
    for b in range(n_batch):
        c = PAIR * b
        out_t = jnp.concatenate([final[3 * c + 2] / final[3 * c + 1], final[3 * c + 5] / final[3 * c + 4]],
                                axis=0)
        o_ref[b] = (out_t.T * g_ref[b]).astype(BF16)


def _moba_prompt(qt, kb, vtb, kmean, bias, gates, batch, seq):
    tq = MOBA_BLOCK
    nq = seq // tq
    n_rows = -(-nq // SUBLANES) * SUBLANES
    g_off = D_SB // LANES
    by_batch = lambda a: a.reshape((batch, a.shape[0] // batch) + a.shape[1:])
    return pl.pallas_call(
        _moba_prompt_body,
        grid=(D_MB // LANES, nq),
        in_specs=[pl.BlockSpec((batch, 1, LANES, tq), lambda p, i: (0, i, p, 0)),
                  pl.BlockSpec((batch, seq, LANES), lambda p, i: (0, 0, p)),
                  pl.BlockSpec((batch, nq, LANES, tq), lambda p, i: (0, 0, p, 0)),
                  pl.BlockSpec((batch, LANES, LANES), lambda p, i: (0, p, 0)),
                  pl.BlockSpec((PAIR, 2, tq, tq), lambda p, i: (p, 0, 0, 0)),
                  pl.BlockSpec((batch, tq, LANES), lambda p, i: (0, i, p + g_off))],
        out_specs=pl.BlockSpec((batch, tq, LANES), lambda p, i: (0, i, p)),
        out_shape=jax.ShapeDtypeStruct((batch, seq, D_MB), BF16),
        scratch_shapes=[pltpu.VMEM((batch * PAIR, n_rows, tq), F32)],
        compiler_params=_cparams(2),
        name="moba_prompt",
    )(by_batch(qt), by_batch(kb), by_batch(vtb), kmean, bias, by_batch(gates)).reshape(batch * seq, D_MB)


def _mem_prompt_body(q_ref, mkt_ref, mvt_ref, g_ref, o_ref):
    tq = q_ref.shape[0]
    for pr in range(D_MEM // LANES):
        chans = slice(pr * LANES, (pr + 1) * LANES)
        q = q_ref[:, chans]
        mkt = mkt_ref[0, chans, :]
        mvt = mvt_ref[0, chans, :]
        out = jnp.zeros((tq, LANES), F32)
        for hh in range(PAIR):
            head = _head_lane_mask((tq, LANES), hh)
            s = _dot(jnp.where(head, q, jnp.zeros_like(q)), mkt)
            p = jnp.exp(s - jnp.max(s, axis=1, keepdims=True))
            o = _dot_nt(p.astype(BF16), mvt) / jnp.sum(p, axis=1, keepdims=True)
            out = jnp.where(head, o, out)
        o_ref[:, chans] = (out * g_ref[:, chans]).astype(BF16)


def _mem_prompt(qb, mktb, mvtb, gates, batch, seq, tq):
    nq = seq // tq
    n_mem = mktb.shape[2]
    g_off = (D_SB + D_MB) // D_MEM
    mem = pl.BlockSpec((1, D_MEM, n_mem), lambda b, i: (b, 0, 0))
    return pl.pallas_call(
        _mem_prompt_body,
        grid=(batch, nq),
        in_specs=[pl.BlockSpec((tq, D_MEM), lambda b, i: (b * nq + i, 0)), mem, mem,
                  pl.BlockSpec((tq, D_MEM), lambda b, i: (b * nq + i, g_off))],
        out_specs=pl.BlockSpec((tq, D_MEM), lambda b, i: (b * nq + i, 0)),
        out_shape=jax.ShapeDtypeStruct((batch * seq, D_MEM), BF16),
        compiler_params=_cparams(2),
        name="mem_prompt",
    )(qb, mktb, mvtb, gates)


def _head_rows(x_row, width):
    r = lax.broadcasted_iota(jnp.int32, (SUBLANES, width), 0)
    lane = lax.broadcasted_iota(jnp.int32, (SUBLANES, width), 1)
    keep = (lane >= r * HEAD_DIM) & (lane < (r + 1) * HEAD_DIM)
    return jnp.where(keep, jnp.broadcast_to(x_row.astype(F32), (SUBLANES, width)), 0.0), keep


def _collapse_heads(x, keep):
    return jnp.sum(jnp.where(keep, x, 0.0), axis=0, keepdims=True)


def _sb_sample_chain(qm, u, k_refs, v_refs, carry, acc):
    n = len(k_refs)
    return _sb_blocks([qm] * n, [k_ref[0, 0].astype(BF16) for k_ref in reversed(k_refs)],
                      [v_ref[0, 0].astype(BF16) for v_ref in reversed(v_refs)], u, [(carry, acc)], [0] * n)[0]


def _sb_sample_head_body(pt_ref, q_ref, g_ref, u_ref, *refs, n_pages, group):
    del pt_ref
    n = n_pages * group
    k_refs, v_refs = refs[:n], refs[n:2 * n]
    o_ref, acc_ref, carry_ref, more_ref = refs[2 * n:]
    rows = [_head_rows(q_ref[g], D_SB) for g in range(group)]
    order = [(g, g * n_pages + p) for p in reversed(range(n_pages)) for g in range(group)]
    state = _sb_blocks([rows[g][0].astype(BF16) for g, _ in order],
                       [k_refs[i][0, 0].astype(BF16) for _, i in order],
                       [v_refs[i][0, 0].astype(BF16) for _, i in order], u_ref[...],
                       [(jnp.zeros((SUBLANES, 1), F32), jnp.zeros((SUBLANES, D_SB), F32))] * group,
                       [g for g, _ in order])
    live = lax.broadcasted_iota(jnp.int32, (SUBLANES, 1), 0) < H_SB
    for g, (carry, acc) in enumerate(state):
        o_ref[g] = (_collapse_heads(acc, rows[g][1]) * g_ref[g]).astype(BF16)
        acc_ref[g] = acc
        carry_ref[g] = jnp.broadcast_to(carry, (SUBLANES, LANES))
        more = jnp.max(jnp.where(live, carry, -jnp.inf)) > -SB_EXIT
        more_ref[g] = jnp.broadcast_to(more.astype(jnp.int32), (SUBLANES, LANES))


def _sb_sample_tail_body(pt_ref, more_ref, q_ref, g_ref, u_ref, acc_ref, carry_ref, *refs, n_pages):
    del pt_ref
    b = pl.program_id(0)
    k_refs, v_refs, o_ref = refs[:n_pages], refs[n_pages:2 * n_pages], refs[2 * n_pages]
    qm, keep = _head_rows(q_ref[0], D_SB)
    acc0 = acc_ref[0]

    def rest():
        return _sb_sample_chain(qm.astype(BF16), u_ref[...], k_refs, v_refs, carry_ref[0][:, 0:1], acc0)[1]

    acc = lax.cond(more_ref[b] != 0, rest, lambda: acc0)
    o_ref[0] = (_collapse_heads(acc, keep) * g_ref[0]).astype(BF16)


def _page_specs(layer, pages, width, gated=False):
    def spec(p):
        if gated:
            return pl.BlockSpec((1, 1, width, PAGE_SIZE),
                                lambda b, pt, more: (layer, jnp.where(more[b] != 0, pt[b, p], 0), 0, 0))
        return pl.BlockSpec((1, 1, width, PAGE_SIZE), lambda b, pt: (layer, pt[b, p], 0, 0))
    return [spec(p) for p in pages]


def _row_spec(width, off=0):
    return pl.BlockSpec((1, 1, width), lambda b, *_: (b, 0, off))


def _sb_sample(layer, page_table, qb, gates3, u, cache_kt, cache_vt):
    db, n_pages = page_table.shape
    group = SAMPLE_GROUP
    q3 = qb.reshape(db, 1, D_SB)
    head_pages = range(n_pages - SB_SAMPLE_HEAD_PAGES, n_pages)
    tail_pages = range(0, n_pages - SB_SAMPLE_HEAD_PAGES)
    u_spec = pl.BlockSpec((PAGE_SIZE, PAGE_SIZE), lambda b, *_: (0, 0))
    rows = lambda w, n=1: pl.BlockSpec((group, n, w), lambda b, *_: (b, 0, 0))
    group_pages = [pl.BlockSpec((1, 1, D_SB, PAGE_SIZE),
                                functools.partial(lambda b, pt, g, p: (layer, pt[b * group + g, p], 0, 0), g=g, p=p))
                   for g in range(group) for p in head_pages]
    n_head = len(group_pages)
    out, acc, carry, more = pl.pallas_call(
        functools.partial(_sb_sample_head_body, n_pages=len(head_pages), group=group),
        grid_spec=pltpu.PrefetchScalarGridSpec(
            num_scalar_prefetch=1,
            grid=(db // group,),
            in_specs=[rows(D_SB), rows(D_SB), u_spec] + group_pages + group_pages,
            out_specs=[rows(D_SB), rows(D_SB, SUBLANES), rows(LANES, SUBLANES), rows(LANES, SUBLANES)],
        ),
        out_shape=[jax.ShapeDtypeStruct((db, 1, D_SB), BF16),
                   jax.ShapeDtypeStruct((db, SUBLANES, D_SB), F32),
                   jax.ShapeDtypeStruct((db, SUBLANES, LANES), F32),
                   jax.ShapeDtypeStruct((db, SUBLANES, LANES), jnp.int32)],
        compiler_params=_cparams(1),
        name="sb_sample_head",
    )(page_table, q3, gates3, u, *([cache_kt] * n_head), *([cache_vt] * n_head))
    more = more[:, 0, 0]

    def tail():
        state = lambda w: pl.BlockSpec((1, SUBLANES, w), lambda b, *_: (b, 0, 0))
        return pl.pallas_call(
            functools.partial(_sb_sample_tail_body, n_pages=len(tail_pages)),
            grid_spec=pltpu.PrefetchScalarGridSpec(
                num_scalar_prefetch=2,
                grid=(db,),
                in_specs=[_row_spec(D_SB), _row_spec(D_SB), u_spec, state(D_SB), state(LANES)]
                         + _page_specs(layer, tail_pages, D_SB, gated=True)
                         + _page_specs(layer, tail_pages, D_SB, gated=True),
                out_specs=_row_spec(D_SB),
            ),
            out_shape=jax.ShapeDtypeStruct((db, 1, D_SB), BF16),
            compiler_params=_cparams(1),
            name="sb_sample_tail",
        )(page_table, more, q3, gates3, u, acc, carry,
          *([cache_kt] * len(tail_pages)), *([cache_vt] * len(tail_pages)))

    return lax.cond(jnp.any(more != 0), tail, lambda: out)


def _moba_sample_body(pt_ref, qf_ref, kn_ref, vn_ref, bias_ref, bown_ref, g_ref, *refs, n_pages, group):
    del pt_ref
    n = n_pages * group
    for g in range(group):
        pages = slice(g * n_pages, (g + 1) * n_pages)
        _moba_sample_one(qf_ref[g], kn_ref[g], vn_ref[g], bias_ref, bown_ref, g_ref[g],
                         refs[:n][pages], refs[n:2 * n][pages], refs[2 * n].at[g], n_pages)


def _moba_sample_one(q_row, k_new, v_new, bias_ref, bown_ref, gate_row, k_refs, v_refs, o_ref, n_pages):
    pages_per_block = MOBA_BLOCK // PAGE_SIZE
    n_past = n_pages // pages_per_block
    qf, keep = _head_rows(q_row, D_MB)
    qs = qf * Q_SCALE
    q_hi = qs.astype(BF16).astype(F32)
    q_parts = jnp.concatenate([q_hi, qs - q_hi], axis=0).astype(BF16)

    scores = []
    for p in range(n_pages):
        both = _dot(q_parts, k_refs[p][0, 0].astype(BF16))
        scores.append(both[:SUBLANES] + both[SUBLANES:])
    gates = [jnp.sum(functools.reduce(jnp.add, scores[n * pages_per_block:(n + 1) * pages_per_block]),
                     axis=1, keepdims=True) * (1.0 / (Q_SCALE * MOBA_BLOCK)) for n in range(n_past)]
    sel = _select_columns(gates)

    kn = k_new.astype(BF16).astype(F32)
    s_own = jnp.sum(qf * Q_SCALE * kn, axis=1, keepdims=True) + bown_ref[:, 0:1]
    masked = [jnp.where(sel[p // pages_per_block], scores[p] + bias_ref[:, p * PAGE_SIZE:(p + 1) * PAGE_SIZE],
                        NEG_INF) for p in range(n_pages)]
    m = jnp.maximum(s_own, jnp.max(functools.reduce(jnp.maximum, masked), axis=1, keepdims=True))
    p_own = jnp.exp(s_own - m)
    weights = [jnp.exp(s - m) for s in masked]
    l = p_own + jnp.sum(functools.reduce(jnp.add, weights), axis=1, keepdims=True)
    acc = p_own * v_new
    for w, v_ref in zip(weights, v_refs):
        acc = acc + _dot_nt(w.astype(BF16), v_ref[0, 0].astype(BF16))
    o_ref[...] = (_collapse_heads(acc / l, keep) * gate_row).astype(BF16)


def _moba_sample(layer, page_table, qf, k_new, v_new, bias_keys, bias_own, gates3, cache_kt, cache_vt):
    db, n_pages = page_table.shape
    past = n_pages * PAGE_SIZE
    group = MOBA_SAMPLE_GROUP
    rows = lambda off=0: pl.BlockSpec((group, 1, D_MB), lambda b, pt: (b, 0, off))
    group_pages = [pl.BlockSpec((1, 1, D_MB, PAGE_SIZE),
                                functools.partial(lambda b, pt, g, p: (layer, pt[b * group + g, p], 0, 0), g=g, p=p))
                   for g in range(group) for p in range(n_pages)]
    grid_spec = pltpu.PrefetchScalarGridSpec(
        num_scalar_prefetch=1,
        grid=(db // group,),
        in_specs=[rows(), rows(), rows(),
                  pl.BlockSpec((SUBLANES, past), lambda b, pt: (0, 0)),
                  pl.BlockSpec((SUBLANES, LANES), lambda b, pt: (0, 0)),
                  rows(D_SB // D_MB)] + group_pages + group_pages,
        out_specs=rows(),
    )
    r3 = lambda a: a.reshape(db, 1, D_MB)
    return pl.pallas_call(
        functools.partial(_moba_sample_body, n_pages=n_pages, group=group),
        grid_spec=grid_spec,
        out_shape=jax.ShapeDtypeStruct((db, 1, D_MB), BF16),
        compiler_params=_cparams(1),
        name="moba_sample",
    )(page_table, r3(qf), r3(k_new), r3(v_new), bias_keys, bias_own, gates3,
      *([cache_kt] * len(group_pages)), *([cache_vt] * len(group_pages)))


def _mem_sample_body(q_ref, mkt_ref, mvt_ref, g_ref, o_ref, *, group):
    rows = [_head_rows(q_ref[g], D_MEM) for g in range(group)]
    scores = [_dot(rows[g][0].astype(BF16), mkt_ref[0, g].astype(BF16)) for g in range(group)]
    probs = [jnp.exp(s - jnp.max(s, axis=1, keepdims=True)) for s in scores]
    outs = [_dot_nt(p.astype(BF16), mvt_ref[0, g].astype(BF16)) for g, p in enumerate(probs)]
    for g in range(group):
        o = outs[g] / jnp.sum(probs[g], axis=1, keepdims=True)
        o_ref[g] = (_collapse_heads(o, rows[g][1]) * g_ref[g]).astype(BF16)


def _mem_sample(layer, qb, gates3, mem_kt, mem_vt):
    db, n_mem = mem_kt.shape[1], mem_kt.shape[3]
    group = MEM_SAMPLE_GROUP
    mem = pl.BlockSpec((1, group, D_MEM, n_mem), lambda b: (layer, b, 0, 0))
    row = lambda off: pl.BlockSpec((group, 1, D_MEM), lambda b: (b, 0, off))
    return pl.pallas_call(
        functools.partial(_mem_sample_body, group=group),
        grid=(db // group,),
        in_specs=[row(0), mem, mem, row((D_SB + D_MB) // D_MEM)],
        out_specs=row(0),
        out_shape=jax.ShapeDtypeStruct((db, 1, D_MEM), BF16),
        compiler_params=_cparams(1),
        name="mem_sample",
    )(qb.reshape(db, 1, D_MEM), mem_kt, mem_vt, gates3)


def _upper_ones(n):
    i = np.arange(n)
    return jnp.asarray((i[:, None] >= i[None, :]).astype(np.float32), dtype=BF16)


def _channel_major(cache, heads):
    lead = cache.shape[:-3]
    t = jnp.moveaxis(cache, -3, -1)
    return t.reshape(lead + (heads * HEAD_DIM, cache.shape[-3]))


def _token_major(x, heads):
    lead = x.shape[:-2]
    return jnp.moveaxis(x.reshape(lead + (heads, HEAD_DIM, x.shape[-1])), -1, -3)


def kernel(x_prompt, x_sample, cache_sb_k, cache_sb_v, cache_moba_k, cache_moba_v, cache_mem_k,
           cache_mem_v, page_table, mem_prompt, rel_bias, g_pre, g_post, g_mem, w_in, w_out, w_mem_kv):
    batch, seq, d_model = x_prompt.shape
    db, dec_seq, _ = x_sample.shape
    depth = w_in.shape[0]
    n_mem = mem_prompt.shape[1]
    n_pages = page_table.shape[1]
    past = n_pages * PAGE_SIZE
    assert dec_seq == 1 and seq % TOKEN_TILE == 0 and past % MOBA_BLOCK == 0
    assert past >= 2 * MOBA_BLOCK and db % SAMPLE_GROUP == 0 and db % MEM_SAMPLE_GROUP == 0 and db % MOBA_SAMPLE_GROUP == 0

    tm = TOKEN_TILE
    tiles = seq // tm
    u_block = _upper_ones(TOKEN_TILE)
    u_page = _upper_ones(PAGE_SIZE)
    p_outs, p_plan = _prompt_proj_plan(depth)
    s_outs, s_plan = _sample_proj_plan()
    kv_outs, kv_plan = _mem_kv_plan()

    bias = _bias_tiles(rel_bias)
    pad = jnp.zeros((SUBLANES - H_MB, past), F32)
    bias_keys = jnp.concatenate(
        [jnp.concatenate([jnp.zeros((H_MB, past - MOBA_BLOCK), F32), bias[:, 1, :, 0]], axis=1), pad], axis=0)
    bias_own = jnp.concatenate(
        [jnp.broadcast_to(bias[:, 0, 0, 0:1], (H_MB, LANES)), pad[:, :LANES]], axis=0)

    sb_kt, sb_vt = _channel_major(cache_sb_k, H_SB), _channel_major(cache_sb_v, H_SB)
    mb_kt, mb_vt = _channel_major(cache_moba_k, H_MB), _channel_major(cache_moba_v, H_MB)
    mem_kt, mem_vt = _channel_major(cache_mem_k, H_MEM), _channel_major(cache_mem_v, H_MEM)

    xp = x_prompt.reshape(batch * seq, d_model)
    xs = x_sample.reshape(db, d_model)
    mem2 = mem_prompt.reshape(batch * n_mem, d_model)
    pt = page_table.astype(jnp.int32)
    outs = [[] for _ in range(10)]
    kv_layers = None
    for l in range(depth):
        w_in_l = w_in[l].astype(BF16)
        w_kv_t = _kv_weight_t(w_in[l])
        w_out_l = w_out[l].astype(BF16)

        *kv_layers, ktb_sb, vtb_sb, vtb_mb, qt_mb, kb_mb, qb_sb, qb_mem, gates, kmean = _norm_proj(
            xp, g_pre[l], w_in_l, w_kv_t, p_outs, p_plan, tm, tiles, layer=l, carried=kv_layers)
        mkt, mvt, mktb, mvtb = _norm_proj(mem2, g_mem[l], None, w_mem_kv[l].T.astype(BF16),
                                          kv_outs, kv_plan, n_mem, 1)
        m_sb = _sb_prompt(qb_sb, ktb_sb, vtb_sb, gates, u_block, batch, seq)
        m_mb = _moba_prompt(qt_mb, kb_mb, vtb_mb, kmean, bias, gates, batch, seq)
        m_mem = _mem_prompt(qb_mem, mktb, mvtb, gates, batch, seq, 2 * tm)
        xp = _merge(m_sb, m_mb, m_mem, xp, w_out_l, g_post[l], 2 * tm)
        outs[4].append(_token_major(mkt, H_MEM))
        outs[5].append(_token_major(mvt, H_MEM))

        (kt_sb, vt_sb, kt_mb, vt_mb, k_mb, v_mb, qb_sb, qf_mb, qb_mem, gates) = _norm_proj(
            xs, g_pre[l], w_in_l, w_kv_t, s_outs, s_plan, db, 1)
        gates3 = gates.reshape(db, 1, D_MIX)
        s_sb = _sb_sample(l, pt, qb_sb, gates3, u_page, sb_kt, sb_vt)
        s_mb = _moba_sample(l, pt, qf_mb, k_mb, v_mb, bias_keys, bias_own, gates3, mb_kt, mb_vt)
        s_mem = _mem_sample(l, qb_mem, gates3, mem_kt, mem_vt)
        xs = _merge(s_sb.reshape(db, D_SB), s_mb.reshape(db, D_MB), s_mem.reshape(db, D_MEM),
                    xs, w_out_l, g_post[l], db)
        for i, (a, h) in enumerate(((kt_sb, H_SB), (vt_sb, H_SB), (kt_mb, H_MB), (vt_mb, H_MB))):
            outs[6 + i].append(_token_major(a, h).reshape(db, 1, h, HEAD_DIM))

    prompt_kv = [_token_major(a, h) for a, h in zip(kv_layers, (H_SB, H_SB, H_MB, H_MB))]
    return ((xp.reshape(batch, seq, d_model), xs.reshape(db, 1, d_model)) + tuple(prompt_kv)
            + tuple(jnp.stack(o) for o in outs[4:]))
```
